```python
import math
import jax
import jax.numpy as jnp
from jax import lax
import numpy as np

D_MODEL = 1024
BATCH = 2
SEQ = 8192
DEPTH = 2
DEC_BATCH = 32
DEC_SEQ = 1
PAST_LEN = 8192
PAGE_SIZE = 128

N_MIXERS = 2
N_ATTN_LAYERS = (DEPTH + 1) // 2
N_DELTA_LAYERS = DEPTH // 2
ATTN_HEADS = 16
ATTN_HEAD_DIM = D_MODEL // ATTN_HEADS
ATTN_DIM = ATTN_HEADS * ATTN_HEAD_DIM
MOBA_BLOCK = 256
MOBA_TOPK = 3
Q_BLOCK = 128
ROPE_THETA = 10000.0
DN_K_HEADS = 8
DN_V_HEADS = 16
DN_HEAD_K = 128
DN_HEAD_V = 128
DN_KEY_DIM = DN_K_HEADS * DN_HEAD_K
DN_VAL_DIM = DN_V_HEADS * DN_HEAD_V
DN_CONV_DIM = 2 * DN_KEY_DIM + DN_VAL_DIM
DN_IN_DIM = DN_CONV_DIM + DN_VAL_DIM + 2 * DN_V_HEADS
CONV_W = 4
DN_CHUNK = 64
FFN_HIDDEN = -(-8 * D_MODEL // (3 * 256)) * 256
NORM_EPS = 1e-6

kernel_name = 'hybrid_moba_gated_deltanet_step'


def rms_norm(x, w):
    xf = x.astype(jnp.float32)
    y = xf * lax.rsqrt(jnp.mean(xf * xf, axis=-1, keepdims=True) + NORM_EPS)
    return (y * w.astype(jnp.float32)).astype(x.dtype)


def l2_normalize(x):
    xf = x.astype(jnp.float32)
    return xf * lax.rsqrt(jnp.sum(xf * xf, axis=-1, keepdims=True) + NORM_EPS)


def rotary(x, pos):
    half = x.shape[-1] // 2
    inv_freq = ROPE_THETA ** (-jnp.arange(half, dtype=jnp.float32) / half)
    ang = pos.astype(jnp.float32)[:, None] * inv_freq[None, :]
    cos = jnp.cos(ang)[None, :, None, :]
    sin = jnp.sin(ang)[None, :, None, :]
    xf = x.astype(jnp.float32)
    x1, x2 = xf[..., :half], xf[..., half:]
    return jnp.concatenate([x1 * cos - x2 * sin, x2 * cos + x1 * sin], axis=-1).astype(x.dtype)


def swiglu(h, w_in, w_out):
    gate, up = jnp.split(h @ w_in, 2, axis=-1)
    return (jax.nn.silu(gate) * up) @ w_out


def gather_pages(pool, page_table):
    rows = pool[page_table]
    return rows.reshape((rows.shape[0], rows.shape[1] * rows.shape[2]) + pool.shape[2:])


def moba_attend(q, q_pos, kb, vb, kmean):
    b, h, t, _ = q.shape
    nb = kb.shape[1]
    cur = q_pos // MOBA_BLOCK
    gate = jnp.einsum('bhtd,bnhd->bhtn', q.astype(jnp.float32), kmean)
    fully_past = jnp.arange(nb)[None, :] < cur[:, None]
    gate = jnp.where(fully_past[None, None], gate, -jnp.inf)
    _, sel = lax.top_k(gate, min(MOBA_TOPK, nb))
    sel_ok = sel < cur[None, None, :, None]
    own = jnp.broadcast_to(cur[None, None, :, None], (b, h, t, 1)).astype(sel.dtype)
    blocks = jnp.concatenate([sel, own], axis=-1)
    ok = jnp.concatenate([sel_ok, jnp.ones((b, h, t, 1), bool)], axis=-1)
    gather = jax.vmap(jax.vmap(lambda kbh, ix: kbh[ix], in_axes=(2, 0)), in_axes=(0, 0))
    k_sel = gather(kb, blocks)
    v_sel = gather(vb, blocks)
    key_pos = blocks[..., None] * MOBA_BLOCK + jnp.arange(MOBA_BLOCK)
    mask = ok[..., None] & (key_pos <= q_pos[None, None, :, None, None])
    s = jnp.einsum('bhtd,bhtjsd->bhtjs', q, k_sel).astype(jnp.float32) * (q.shape[-1] ** -0.5)
    s = jnp.where(mask, s, -jnp.inf).reshape(b, h, t, -1)
    p = jax.nn.softmax(s, axis=-1).reshape(mask.shape).astype(v_sel.dtype)
    return jnp.einsum('bhtjs,bhtjsd->bhtd', p, v_sel)


def moba_attention(h, past_k, past_v, w_qkv, w_o):
    b, t, _ = h.shape
    past_len = past_k.shape[1]
    qkv = (h @ w_qkv).reshape(b, t, 3, ATTN_HEADS, ATTN_HEAD_DIM)
    pos = past_len + jnp.arange(t, dtype=jnp.int32)
    q = rotary(qkv[:, :, 0], pos)
    k = rotary(qkv[:, :, 1], pos)
    v = qkv[:, :, 2]
    total = past_len + t
    nb = -(-total // MOBA_BLOCK)
    pad = jnp.zeros((b, nb * MOBA_BLOCK - total, ATTN_HEADS, ATTN_HEAD_DIM), k.dtype)
    kb = jnp.concatenate([past_k.astype(k.dtype), k, pad], axis=1).reshape(b, nb, MOBA_BLOCK, ATTN_HEADS, ATTN_HEAD_DIM)
    vb = jnp.concatenate([past_v.astype(v.dtype), v, pad.astype(v.dtype)], axis=1).reshape(b, nb, MOBA_BLOCK, ATTN_HEADS, ATTN_HEAD_DIM)
    kmean = jnp.mean(kb, axis=2, dtype=jnp.float32)
    qb = Q_BLOCK if t % Q_BLOCK == 0 else t
    nq = t // qb
    q_blocks = q.reshape(b, nq, qb, ATTN_HEADS, ATTN_HEAD_DIM).transpose(1, 0, 3, 2, 4)
    pos_blocks = pos.reshape(nq, qb)
    o = lax.map(lambda a: moba_attend(a[0], a[1], kb, vb, kmean), (q_blocks, pos_blocks))
    o = o.transpose(1, 0, 3, 2, 4).reshape(b, t, ATTN_DIM)
    return o @ w_o, k, v


def gated_delta_rule(q, k, v, g, beta, s0, chunk):
    b, t, h, dk = q.shape
    dv = v.shape[-1]
    nc = t // chunk

    def to_chunks(x):
        return jnp.moveaxis(x.reshape((b, nc, chunk) + x.shape[2:]), 2, 3)

    qc, kc, vc, gc, bc = (to_chunks(a) for a in (q, k, v, g, beta))
    G = jnp.cumsum(gc, axis=-1)
    idx = jnp.arange(chunk)
    causal = idx[:, None] >= idx[None, :]
    strict = idx[:, None] > idx[None, :]
    decay = jnp.where(causal, jnp.exp(jnp.where(causal, G[..., :, None] - G[..., None, :], 0.0)), 0.0)
    kk = jnp.einsum('bnhck,bnhsk->bnhcs', kc, kc)
    A = jnp.where(strict, bc[..., :, None] * kk * decay, 0.0)
    rhs = jnp.concatenate([bc[..., None] * vc, (bc * jnp.exp(G))[..., None] * kc], axis=-1)
    sol = lax.linalg.triangular_solve(A, rhs, left_side=True, lower=True, unit_diagonal=True)
    u_val, w = sol[..., :dv], sol[..., dv:]
    qk = jnp.einsum('bnhck,bnhsk->bnhcs', qc, kc) * decay
    q_dec = qc * jnp.exp(G)[..., None]
    k_tail = kc * jnp.exp(G[..., -1:] - G)[..., None]
    tail = jnp.exp(G[..., -1])

    def step(S, xs):
        u_val_c, w_c, qk_c, q_dec_c, k_tail_c, tail_c = xs
        u = u_val_c - jnp.einsum('bhck,bhkv->bhcv', w_c, S)
        o = jnp.einsum('bhck,bhkv->bhcv', q_dec_c, S) + jnp.einsum('bhcs,bhsv->bhcv', qk_c, u)
        S = S * tail_c[..., None, None] + jnp.einsum('bhck,bhcv->bhkv', k_tail_c, u)
        return S, o

    xs = tuple(jnp.moveaxis(a, 1, 0) for a in (u_val, w, qk, q_dec, k_tail, tail))
    s_final, o = lax.scan(step, s0, xs)
    o = jnp.moveaxis(jnp.moveaxis(o, 0, 1), 3, 2).reshape(b, t, h, dv)
    return o, s_final


def gated_deltanet(h, conv_hist, s0, w_in, conv_w, a_log, dt_bias, gnorm_w, w_out):
    b, t, _ = h.shape
    proj = h @ w_in
    o1 = DN_CONV_DIM
    o2 = o1 + DN_VAL_DIM
    o3 = o2 + DN_V_HEADS
    qkv_in, z, b_in, a_in = proj[..., :o1], proj[..., o1:o2], proj[..., o2:o3], proj[..., o3:]
    xc = jnp.concatenate([conv_hist.astype(qkv_in.dtype), qkv_in], axis=1)
    conv = xc[:, 0:t] * conv_w[0]
    for i in range(1, CONV_W):
        conv = conv + xc[:, i:i + t] * conv_w[i]
    qkv = jax.nn.silu(conv)
    new_hist = xc[:, t:]
    rep = DN_V_HEADS // DN_K_HEADS
    q = jnp.repeat(l2_normalize(qkv[..., :DN_KEY_DIM].reshape(b, t, DN_K_HEADS, DN_HEAD_K)), rep, axis=2) * (DN_HEAD_K ** -0.5)
    k = jnp.repeat(l2_normalize(qkv[..., DN_KEY_DIM:2 * DN_KEY_DIM].reshape(b, t, DN_K_HEADS, DN_HEAD_K)), rep, axis=2)
    v = qkv[..., 2 * DN_KEY_DIM:].reshape(b, t, DN_V_HEADS, DN_HEAD_V).astype(jnp.float32)
    beta = jax.nn.sigmoid(b_in.astype(jnp.float32))
    g = -jnp.exp(a_log.astype(jnp.float32)) * jax.nn.softplus(a_in.astype(jnp.float32) + dt_bias.astype(jnp.float32))
    chunk = DN_CHUNK if t % DN_CHUNK == 0 else t
    o, s_new = gated_delta_rule(q, k, v, g, beta, s0.astype(jnp.float32), chunk)
    zf = z.reshape(b, t, DN_V_HEADS, DN_HEAD_V).astype(jnp.float32)
    o = o * lax.rsqrt(jnp.mean(o * o, axis=-1, keepdims=True) + NORM_EPS) * gnorm_w.astype(jnp.float32) * jax.nn.silu(zf)
    y = o.reshape(b, t, DN_VAL_DIM).astype(h.dtype) @ w_out
    return y, new_hist, s_new.astype(s0.dtype)


def setup_inputs(seed: int = 0) -> dict:
    key = jax.random.key(seed)
    ks = jax.random.split(key, 20)
    f32 = jnp.float32
    n_pages = PAST_LEN // PAGE_SIZE
    n_used = DEC_BATCH * n_pages
    n_phys = (5 * n_used + 3) // 4

    def normal(k, shape, scale=1.0):
        return jax.random.normal(k, shape, f32) * scale

    def gain(k, shape):
        return 1.0 + 0.01 * jax.random.normal(k, shape, f32)

    page_table = jax.random.permutation(ks[4], n_phys)[:n_used].reshape(DEC_BATCH, n_pages).astype(jnp.int32)
    a_log = jnp.log(jax.random.uniform(ks[11], (N_DELTA_LAYERS, DN_V_HEADS), f32, 1.0, 16.0))
    dt = jnp.exp(jax.random.uniform(ks[12], (N_DELTA_LAYERS, DN_V_HEADS), f32, math.log(1e-3), math.log(1e-1)))
    dt_bias = dt + jnp.log(-jnp.expm1(-dt))
    return {
        'x_prompt': normal(ks[0], (BATCH, SEQ, D_MODEL)),
        'x_sample': normal(ks[1], (DEC_BATCH, DEC_SEQ, D_MODEL)),
        'cache_k': normal(ks[2], (N_ATTN_LAYERS, n_phys, PAGE_SIZE, ATTN_HEADS, ATTN_HEAD_DIM)),
        'cache_v': normal(ks[3], (N_ATTN_LAYERS, n_phys, PAGE_SIZE, ATTN_HEADS, ATTN_HEAD_DIM)),
        'page_table': page_table,
        'state_conv': normal(ks[5], (N_DELTA_LAYERS, DEC_BATCH, CONV_W - 1, DN_CONV_DIM)),
        'state_ssm': normal(ks[6], (N_DELTA_LAYERS, DEC_BATCH, DN_V_HEADS, DN_HEAD_K, DN_HEAD_V), 0.1),
        'norm_mix': gain(ks[7], (DEPTH, D_MODEL)),
        'w_qkv_attn': normal(ks[8], (N_ATTN_LAYERS, D_MODEL, 3 * ATTN_DIM), D_MODEL ** -0.5),
        'w_o_attn': normal(ks[9], (N_ATTN_LAYERS, ATTN_DIM, D_MODEL), ATTN_DIM ** -0.5),
        'w_in_delta': normal(ks[10], (N_DELTA_LAYERS, D_MODEL, DN_IN_DIM), D_MODEL ** -0.5),
        'conv_w_delta': normal(ks[13], (N_DELTA_LAYERS, CONV_W, DN_CONV_DIM), CONV_W ** -0.5),
        'a_log_delta': a_log,
        'dt_bias_delta': dt_bias,
        'gnorm_delta': gain(ks[14], (N_DELTA_LAYERS, DN_HEAD_V)),
        'w_out_delta': normal(ks[15], (N_DELTA_LAYERS, DN_VAL_DIM, D_MODEL), DN_VAL_DIM ** -0.5),
        'norm_ffn': gain(ks[16], (DEPTH, D_MODEL)),
        'w_ffn_in': normal(ks[17], (DEPTH, D_MODEL, 2 * FFN_HIDDEN), D_MODEL ** -0.5),
        'w_ffn_out': normal(ks[18], (DEPTH, FFN_HIDDEN, D_MODEL), FFN_HIDDEN ** -0.5),
        'norm_final': gain(ks[19], (D_MODEL,)),
    }


def reference(x_prompt, x_sample, cache_k, cache_v, page_table, state_conv, state_ssm,
              norm_mix, w_qkv_attn, w_o_attn, w_in_delta, conv_w_delta, a_log_delta,
              dt_bias_delta, gnorm_delta, w_out_delta, norm_ffn, w_ffn_in, w_ffn_out, norm_final):
    b = x_prompt.shape[0]
    xp, xs = x_prompt, x_sample
    k_p, v_p, k_s, v_s = [], [], [], []
    conv_p, ssm_p, conv_s, ssm_s = [], [], [], []
    for i in range(DEPTH):
        j = i // N_MIXERS
        hp = rms_norm(xp, norm_mix[i])
        hs = rms_norm(xs, norm_mix[i])
        if i % N_MIXERS == 0:
            empty = jnp.zeros((b, 0, ATTN_HEADS, ATTN_HEAD_DIM), hp.dtype)
            yp, kp, vp = moba_attention(hp, empty, empty, w_qkv_attn[j], w_o_attn[j])
            past_k = gather_pages(cache_k[j], page_table)
            past_v = gather_pages(cache_v[j], page_table)
            ys, ks_new, vs_new = moba_attention(hs, past_k, past_v, w_qkv_attn[j], w_o_attn[j])
            k_p.append(kp)
            v_p.append(vp)
            k_s.append(ks_new)
            v_s.append(vs_new)
        else:
            hist0 = jnp.zeros((b, CONV_W - 1, DN_CONV_DIM), hp.dtype)
            s00 = jnp.zeros((b, DN_V_HEADS, DN_HEAD_K, DN_HEAD_V), state_ssm.dtype)
            yp, cp, sp = gated_deltanet(hp, hist0, s00, w_in_delta[j], conv_w_delta[j], a_log_delta[j],
                                        dt_bias_delta[j], gnorm_delta[j], w_out_delta[j])
            ys, cs, ss = gated_deltanet(hs, state_conv[j], state_ssm[j], w_in_delta[j], conv_w_delta[j],
                                        a_log_delta[j], dt_bias_delta[j], gnorm_delta[j], w_out_delta[j])
            conv_p.append(cp)
            ssm_p.append(sp)
            conv_s.append(cs)
            ssm_s.append(ss)
        xp = xp + yp
        xs = xs + ys
        xp = xp + swiglu(rms_norm(xp, norm_ffn[i]), w_ffn_in[i], w_ffn_out[i])
        xs = xs + swiglu(rms_norm(xs, norm_ffn[i]), w_ffn_in[i], w_ffn_out[i])
    y_prompt = rms_norm(xp, norm_final)
    y_sample = rms_norm(xs, norm_final)
    return (y_prompt, y_sample, jnp.stack(k_p), jnp.stack(v_p), jnp.stack(k_s), jnp.stack(v_s),
            jnp.stack(conv_p), jnp.stack(ssm_p), jnp.stack(conv_s), jnp.stack(ssm_s))
```

```python
import functools
import math

import jax
import jax.numpy as jnp
from jax import lax
from jax.experimental import pallas as pl
from jax.experimental.pallas import tpu as pltpu

F32 = jnp.float32
BF16 = jnp.bfloat16

NORM_EPS = 1e-6
ROPE_THETA = 10000.0
MOBA_BLOCK = 256
MOBA_TOPK = 3
DN_CHUNK = 64
INV_BASE = 8
VMEM_LIMIT_BYTES = 48 * 1024 * 1024
NEG_INF = float("-inf")
HI = lax.Precision.HIGHEST


def _params(*sem):
    return pltpu.CompilerParams(dimension_semantics=sem, vmem_limit_bytes=VMEM_LIMIT_BYTES)


def _rms(x, w):
    ms = jnp.mean(x * x, axis=-1, keepdims=True)
    return x * lax.rsqrt(ms + NORM_EPS) * w


def _bdot(a, b):
    return jnp.dot(a.astype(BF16), b.astype(BF16), preferred_element_type=F32)


def _bdot_nt(a, b):
    return lax.dot_general(a.astype(BF16), b.astype(BF16), (((1,), (1,)), ((), ())),
                           preferred_element_type=F32)


def _silu(x):
    return x * jax.nn.sigmoid(x)


def _row_tile(m, want):
    return want if m % want == 0 else m


def _qkv_rope_kernel(x_ref, nw_ref, w_ref, cos_ref, sa_ref, sb_ref,
                     q_ref, k_ref, v_ref, kb_ref, vb_ref, *km_ref, d, half, q_scale):
    h = _rms(x_ref[...], nw_ref[...]).astype(BF16)
    rep = d // cos_ref.shape[-1]
    cos = jnp.concatenate([cos_ref[...]] * rep, axis=1)
    sa = jnp.concatenate([sa_ref[...]] * rep, axis=1)
    sb = jnp.concatenate([sb_ref[...]] * rep, axis=1)

    def rope(t):
        return t * cos + pltpu.roll(t, d - half, 1) * sa + pltpu.roll(t, half, 1) * sb

    q = rope(jnp.dot(h, w_ref[:, 0:d], preferred_element_type=F32))
    k = rope(jnp.dot(h, w_ref[:, d:2 * d], preferred_element_type=F32))
    v = jnp.dot(h, w_ref[:, 2 * d:3 * d], preferred_element_type=F32)
    q_ref[...] = (q * q_scale).astype(q_ref.dtype)
    k_ref[...] = k
    v_ref[...] = v
    kb_ref[...] = k.astype(BF16)
    vb_ref[...] = v.astype(BF16)
    if km_ref:
        km_ref[0][0] = jnp.mean(k, axis=0, keepdims=True)


def _rope_tables(pos, head_dim, lanes):
    half = head_dim // 2
    inv_freq = ROPE_THETA ** (-jnp.arange(half, dtype=F32) / half)
    ang = pos.astype(F32)[:, None] * inv_freq[None, :]
    reps = lanes // half
    cos = jnp.tile(jnp.cos(ang), (1, reps))
    sin = jnp.tile(jnp.sin(ang), (1, reps))
    first = (jnp.arange(lanes) % head_dim) < half
    return cos, jnp.where(first, -sin, 0.0), jnp.where(first, 0.0, sin)


def _qkv_rope(x, nw, w_bf, pos, *, tm, head_dim, q_dtype, with_kmean):
    m, d = x.shape
    n_pos = pos.shape[0]
    lanes = 2 * head_dim
    cos, sa, sb = _rope_tables(pos, head_dim, lanes)
    pos_tiles = n_pos // tm
    tab_spec = pl.BlockSpec((tm, lanes), lambda i: (i % pos_tiles, 0))
    row_spec = pl.BlockSpec((tm, d), lambda i: (i, 0))
    out_shape = [jax.ShapeDtypeStruct((m, d), q_dtype), jax.ShapeDtypeStruct((m, d), F32),
                 jax.ShapeDtypeStruct((m, d), F32), jax.ShapeDtypeStruct((m, d), BF16),
                 jax.ShapeDtypeStruct((m, d), BF16)]
    out_specs = [row_spec] * 5
    if with_kmean:
        assert tm == MOBA_BLOCK
        out_shape.append(jax.ShapeDtypeStruct((m // tm, 1, d), F32))
        out_specs.append(pl.BlockSpec((1, 1, d), lambda i: (i, 0, 0)))
    return pl.pallas_call(
        functools.partial(_qkv_rope_kernel, d=d, half=head_dim // 2, q_scale=head_dim ** -0.5),
        grid=(m // tm,),
        in_specs=[row_spec, pl.BlockSpec((1, d), lambda i: (0, 0)),
                  pl.BlockSpec((d, 3 * d), lambda i: (0, 0)), tab_spec, tab_spec, tab_spec],
        out_specs=out_specs, out_shape=out_shape,
        compiler_params=_params("parallel"),
    )(x, nw.reshape(1, d), w_bf, cos, sa, sb)


def _select_blocks(gate, n_past, nb, width):
    blk_id = lax.broadcasted_iota(jnp.int32, (nb, width), 0)
    past = blk_id < n_past
    g = jnp.where(past, gate, NEG_INF)
    sel = jnp.zeros((nb, width), F32)
    for _ in range(min(MOBA_TOPK, nb)):
        top = jnp.max(g, axis=0, keepdims=True)
        idx = jnp.min(jnp.where(g == top, blk_id, nb), axis=0, keepdims=True)
        hit = blk_id == idx
        sel = jnp.where(hit, 1.0, sel)
        g = jnp.where(hit, NEG_INF, g)
    return jnp.where(past, jnp.where(sel > 0.0, 0.0, NEG_INF), NEG_INF)


def _moba_prompt_kernel(q_ref, k_ref, vt_ref, km_ref, o_ref, bias_ref, *, blk, dh):
    qi = pl.program_id(2)
    q = q_ref[0]
    lane = lax.broadcasted_iota(jnp.int32, q.shape, 1)
    nb = km_ref.shape[1]
    kmb = km_ref[0].astype(BF16)
    zero = jnp.zeros_like(q)
    qs = []
    for e in range(2):
        qe = jnp.where((lane >= e * dh) & (lane < (e + 1) * dh), q, zero)
        qs.append(qe)
        bias_ref[e] = _select_blocks(_bdot_nt(kmb, qe), qi, nb, blk)

    kd = k_ref[0, pl.ds(pl.multiple_of(qi * blk, blk), blk), :]
    vtd = vt_ref[0, qi]
    key_i = lax.broadcasted_iota(jnp.int32, (blk, blk), 0)
    qry_i = lax.broadcasted_iota(jnp.int32, (blk, blk), 1)
    carry = []
    for e in range(2):
        s = jnp.where(key_i <= qry_i, _bdot_nt(kd, qs[e]), NEG_INF)
        m = jnp.max(s, axis=0, keepdims=True)
        p = jnp.exp(s - m)
        carry += [m, jnp.sum(p, axis=0, keepdims=True),
                  jnp.dot(vtd[e * dh:(e + 1) * dh, :], p.astype(BF16), preferred_element_type=F32)]

    def body(j, c):
        kj = k_ref[0, pl.ds(pl.multiple_of(j * blk, blk), blk), :]
        vtj = vt_ref[0, j]
        out = []
        for e in range(2):
            m, l, acc = c[3 * e:3 * e + 3]
            s = _bdot_nt(kj, qs[e]) + bias_ref[e, pl.ds(j, 1), :]
            m_new = jnp.maximum(m, jnp.max(s, axis=0, keepdims=True))
            alpha = jnp.exp(m - m_new)
            p = jnp.exp(s - m_new)
            out += [m_new, alpha * l + jnp.sum(p, axis=0, keepdims=True),
                    alpha * acc + jnp.dot(vtj[e * dh:(e + 1) * dh, :], p.astype(BF16),
                                          preferred_element_type=F32)]
        return tuple(out)

    c = lax.fori_loop(0, qi, body, tuple(carry))
    o_t = jnp.concatenate([c[2] / c[1], c[5] / c[4]], axis=0)
    o_ref[0] = o_t.T.astype(o_ref.dtype)


def _moba_prompt(q_bf, k_bf, v_bf, kmean, *, head_dim):
    b, t, d = q_bf.shape
    blk = MOBA_BLOCK
    assert t % blk == 0
    nb = t // blk
    pair = 2 * head_dim
    vt = v_bf.reshape(b, nb, blk, d).transpose(0, 1, 3, 2)
    return pl.pallas_call(
        functools.partial(_moba_prompt_kernel, blk=blk, dh=head_dim),
        grid=(b, d // pair, nb),
        in_specs=[pl.BlockSpec((1, blk, pair), lambda bi, hp, qi: (bi, qi, hp)),
                  pl.BlockSpec((1, t, pair), lambda bi, hp, qi: (bi, 0, hp)),
                  pl.BlockSpec((1, nb, pair, blk), lambda bi, hp, qi: (bi, 0, hp, 0)),
                  pl.BlockSpec((1, nb, pair), lambda bi, hp, qi: (bi, 0, hp))],
        out_specs=pl.BlockSpec((1, blk, pair), lambda bi, hp, qi: (bi, qi, hp)),
        out_shape=jax.ShapeDtypeStruct((b, t, d), BF16),
        scratch_shapes=[pltpu.VMEM((2, nb, blk), F32)],
        compiler_params=_params("parallel", "parallel", "arbitrary"),
    )(q_bf, k_bf, vt, kmean)


def _page_sum_kernel(pt_ref, k_ref, o_ref):
    o_ref[0, 0] = jnp.sum(k_ref[0], axis=0, keepdims=True)


def _page_sums(pool, page_table):
    _, page, d = pool.shape
    b, n_pages = page_table.shape
    return pl.pallas_call(
        _page_sum_kernel,
        grid_spec=pltpu.PrefetchScalarGridSpec(
            num_scalar_prefetch=1, grid=(b, n_pages),
            in_specs=[pl.BlockSpec((1, page, d), lambda bi, p, pt: (pt[bi * n_pages + p], 0, 0))],
            out_specs=pl.BlockSpec((1, 1, 1, d), lambda bi, p, pt: (bi, p, 0, 0))),
        out_shape=jax.ShapeDtypeStruct((b, n_pages, 1, d), F32),
        compiler_params=_params("parallel", "arbitrary"),
    )(page_table.reshape(-1), pool)


def _sample_select_kernel(q_ref, ks_ref, seg_ref, idx_ref, *, nb, n_heads):
    prod = ks_ref[0] * q_ref[0]
    gate = jnp.dot(prod, seg_ref[...], precision=HI, preferred_element_type=F32)
    blk_id = lax.broadcasted_iota(jnp.int32, (nb, n_heads), 0)
    rows = []
    for _ in range(min(MOBA_TOPK, nb)):
        top = jnp.max(gate, axis=0, keepdims=True)
        idx = jnp.min(jnp.where(gate == top, blk_id, nb), axis=0, keepdims=True)
        rows.append(idx)
        gate = jnp.where(blk_id == idx, NEG_INF, gate)
    idx_ref[0] = jnp.concatenate(rows, axis=0)


def _sample_select(q, kmean, n_heads):
    b, nb, d = kmean.shape
    topk = min(MOBA_TOPK, nb)
    seg = (jnp.arange(d)[:, None] // (d // n_heads) == jnp.arange(n_heads)[None, :]).astype(F32)
    return pl.pallas_call(
        functools.partial(_sample_select_kernel, nb=nb, n_heads=n_heads),
        grid=(b,),
        in_specs=[pl.BlockSpec((1, 1, d), lambda bi: (bi, 0, 0)),
                  pl.BlockSpec((1, nb, d), lambda bi: (bi, 0, 0)),
                  pl.BlockSpec((d, n_heads), lambda bi: (0, 0))],
        out_specs=pl.BlockSpec((1, topk, n_heads), lambda bi: (bi, 0, 0)),
        out_shape=jax.ShapeDtypeStruct((b, topk, n_heads), jnp.int32),
        compiler_params=_params("parallel"),
    )(q, kmean, seg)


def _sample_attend_kernel(pg_ref, q_ref, kn_ref, vn_ref, *refs, n_sel, dh):
    k_refs, v_refs, o_ref = refs[:n_sel], refs[n_sel:2 * n_sel], refs[2 * n_sel]
    h = pl.program_id(1)
    lane = lax.broadcasted_iota(jnp.int32, (1, 2 * dh), 1)
    mine = (lane >= (h % 2) * dh) & (lane < (h % 2 + 1) * dh)
    q = jnp.where(mine, q_ref[0], 0.0)
    m = jnp.sum(q * kn_ref[0], axis=-1, keepdims=True)
    l = jnp.ones_like(m)
    acc = vn_ref[0]
    for k_ref, v_ref in zip(k_refs, v_refs):
        s = jnp.sum(k_ref[0] * q, axis=-1, keepdims=True)
        m_new = jnp.maximum(m, jnp.max(s, axis=0, keepdims=True))
        alpha = jnp.exp(m - m_new)
        p = jnp.exp(s - m_new)
        l = alpha * l + jnp.sum(p, axis=0, keepdims=True)
        acc = alpha * acc + jnp.sum(p * v_ref[0], axis=0, keepdims=True)
        m = m_new
    out = acc / l

    @pl.when(h % 2 == 0)
    def _():
        o_ref[0] = jnp.where(mine, out, 0.0)

    @pl.when(h % 2 == 1)
    def _():
        o_ref[0] = jnp.where(mine, out, o_ref[0])


def _sample_attend(q, k_new, v_new, pool_k, pool_v, phys, *, n_heads):
    b, _, d = q.shape
    _, page, _ = pool_k.shape
    dh = d // n_heads
    pair = 2 * dh
    n_sel = phys.shape[-1]
    row_spec = pl.BlockSpec((1, 1, pair), lambda bi, h, pg: (bi, 0, h // 2))

    def page_spec(r):
        return pl.BlockSpec((1, page, pair),
                            lambda bi, h, pg: (pg[(bi * n_heads + h) * n_sel + r], 0, h // 2))

    return pl.pallas_call(
        functools.partial(_sample_attend_kernel, n_sel=n_sel, dh=dh),
        grid_spec=pltpu.PrefetchScalarGridSpec(
            num_scalar_prefetch=1, grid=(b, n_heads),
            in_specs=[row_spec, row_spec, row_spec] + [page_spec(r) for r in range(n_sel)] * 2,
            out_specs=row_spec),
        out_shape=jax.ShapeDtypeStruct((b, 1, d), F32),
        compiler_params=_params("parallel", "arbitrary"),
    )(phys.reshape(-1), q, k_new, v_new, *([pool_k] * n_sel), *([pool_v] * n_sel))


def _matmul_res_kernel(a_ref, w_ref, r_ref, o_ref):
    o_ref[...] = r_ref[...] + _bdot(a_ref[...], w_ref[...])


def _matmul_res(a, w_bf, res, *, tm):
    m, k = a.shape
    n = w_bf.shape[1]
    return pl.pallas_call(
        _matmul_res_kernel, grid=(m // tm,),
        in_specs=[pl.BlockSpec((tm, k), lambda i: (i, 0)), pl.BlockSpec((k, n), lambda i: (0, 0)),
                  pl.BlockSpec((tm, n), lambda i: (i, 0))],
        out_specs=pl.BlockSpec((tm, n), lambda i: (i, 0)),
        out_shape=jax.ShapeDtypeStruct((m, n), F32),
        compiler_params=_params("parallel"),
    )(a, w_bf, res)


def _ffn_kernel(x_ref, nw_ref, wg_ref, wu_ref, wo_ref, *refs, final_norm):
    if final_norm:
        fw_ref, o_ref, h_scr, acc_scr = refs
    else:
        o_ref, h_scr, acc_scr = refs
    j = pl.program_id(1)

    @pl.when(j == 0)
    def _():
        h_scr[...] = _rms(x_ref[...], nw_ref[...]).astype(BF16)
        acc_scr[...] = x_ref[...]

    h = h_scr[...]
    gate = jnp.dot(h, wg_ref[...], preferred_element_type=F32)
    up = jnp.dot(h, wu_ref[...], preferred_element_type=F32)
    acc_scr[...] += _bdot(_silu(gate) * up, wo_ref[...])

    @pl.when(j == pl.num_programs(1) - 1)
    def _():
        y = acc_scr[...]
        o_ref[...] = _rms(y, fw_ref[...]) if final_norm else y


def _ffn(x, nw, w_in_bf, w_out_bf, final_w=None, *, tm, n_hidden_tiles=2):
    m, d = x.shape
    hidden = w_out_bf.shape[0]
    th = hidden // n_hidden_tiles
    assert th * n_hidden_tiles == hidden and th % 128 == 0
    in_specs = [pl.BlockSpec((tm, d), lambda i, j: (i, 0)), pl.BlockSpec((1, d), lambda i, j: (0, 0)),
                pl.BlockSpec((d, th), lambda i, j: (0, j)),
                pl.BlockSpec((d, th), lambda i, j: (0, n_hidden_tiles + j)),
                pl.BlockSpec((th, d), lambda i, j: (j, 0))]
    args = [x, nw.reshape(1, d), w_in_bf, w_in_bf, w_out_bf]
    if final_w is not None:
        in_specs.append(pl.BlockSpec((1, d), lambda i, j: (0, 0)))
        args.append(final_w.reshape(1, d))
    return pl.pallas_call(
        functools.partial(_ffn_kernel, final_norm=final_w is not None),
        grid=(m // tm, n_hidden_tiles), in_specs=in_specs,
        out_specs=pl.BlockSpec((tm, d), lambda i, j: (i, 0)),
        out_shape=jax.ShapeDtypeStruct((m, d), F32),
        scratch_shapes=[pltpu.VMEM((tm, d), BF16), pltpu.VMEM((tm, d), F32)],
        compiler_params=_params("parallel", "arbitrary"),
    )(*args)


def _norm_matmul_kernel(x_ref, nw_ref, w_ref, o_ref, h_scr):
    @pl.when(pl.program_id(1) == 0)
    def _():
        h_scr[...] = _rms(x_ref[...], nw_ref[...]).astype(BF16)

    o_ref[...] = jnp.dot(h_scr[...], w_ref[...], preferred_element_type=F32)


def _norm_matmul(x, nw, w_bf, *, tm, tn):
    m, d = x.shape
    n = w_bf.shape[1]
    return pl.pallas_call(
        _norm_matmul_kernel, grid=(m // tm, n // tn),
        in_specs=[pl.BlockSpec((tm, d), lambda i, j: (i, 0)), pl.BlockSpec((1, d), lambda i, j: (0, 0)),
                  pl.BlockSpec((d, tn), lambda i, j: (0, j))],
        out_specs=pl.BlockSpec((tm, tn), lambda i, j: (i, j)),
        out_shape=jax.ShapeDtypeStruct((m, n), F32),
        scratch_shapes=[pltpu.VMEM((tm, d), BF16)],
        compiler_params=_params("parallel", "arbitrary"),
    )(x, nw.reshape(1, d), w_bf)


def _gates_kernel(x_ref, nw_ref, w_ref, alog_ref, dtb_ref, beta_ref, g_ref, gc_ref, *, nh, chunk):
    h = _rms(x_ref[...], nw_ref[...])
    ba = jnp.dot(h, w_ref[...], precision=HI, preferred_element_type=F32)
    b_in, a_in = ba[:, 0:nh], ba[:, 128:128 + nh]
    z = a_in + dtb_ref[...]
    softplus = jnp.maximum(z, 0.0) + jnp.log1p(jnp.exp(-jnp.abs(z)))
    g = -jnp.exp(alog_ref[...]) * softplus
    beta_ref[...] = jax.nn.sigmoid(b_in)
    g_ref[...] = g
    tm = g.shape[0]
    if chunk == 1:
        gc_ref[...] = g
    else:
        r = lax.broadcasted_iota(jnp.int32, (chunk, chunk), 0)
        c = lax.broadcasted_iota(jnp.int32, (chunk, chunk), 1)
        tri = jnp.where(r >= c, 1.0, 0.0)
        for i in range(tm // chunk):
            gc_ref[i * chunk:(i + 1) * chunk, :] = jnp.dot(
                tri, g[i * chunk:(i + 1) * chunk, :], precision=HI, preferred_element_type=F32)


def _gates(x, nw, w_ba, a_log, dt_bias, *, tm, chunk):
    m, d = x.shape
    nh = a_log.shape[0]
    spec = pl.BlockSpec((tm, nh), lambda i: (i, 0))
    small = pl.BlockSpec((1, nh), lambda i: (0, 0))
    return pl.pallas_call(
        functools.partial(_gates_kernel, nh=nh, chunk=chunk), grid=(m // tm,),
        in_specs=[pl.BlockSpec((tm, d), lambda i: (i, 0)), pl.BlockSpec((1, d), lambda i: (0, 0)),
                  pl.BlockSpec((d, 256), lambda i: (0, 0)), small, small],
        out_specs=[spec, spec, spec],
        out_shape=[jax.ShapeDtypeStruct((m, nh), F32)] * 3,
        compiler_params=_params("parallel"),
    )(x, nw.reshape(1, d), w_ba, a_log.reshape(1, nh), dt_bias.reshape(1, nh))


def _post_conv(conv, q_ref, k_ref, v_ref, *, key_dim, dk):
    y = _silu(conv)
    for j in range(key_dim // dk):
        for ref, off, scale in ((q_ref, 0, dk ** -0.5), (k_ref, key_dim, 1.0)):
            t = y[:, off + j * dk:off + (j + 1) * dk]
            inv = lax.rsqrt(jnp.sum(t * t, axis=-1, keepdims=True) + NORM_EPS)
            ref[0, :, j * dk:(j + 1) * dk] = t * (inv * scale)
    v_ref[0] = y[:, 2 * key_dim:]


def _conv_prompt_kernel(cur_ref, prev_ref, w_ref, q_ref, k_ref, v_ref, *, key_dim, dk, width):
    cur = cur_ref[0]
    tt = cur.shape[0]
    prev = jnp.where(pl.program_id(1) == 0, 0.0, prev_ref[0])
    row = lax.broadcasted_iota(jnp.int32, (8, cur.shape[1]), 0)
    conv = cur * w_ref[width - 1:width, :]
    for s in range(1, width):
        rolled = pltpu.roll(cur, s, 0)
        head = jnp.where(row < s, pltpu.roll(prev, s, 0), rolled[0:8])
        shifted = jnp.concatenate([head, rolled[8:]], axis=0) if tt > 8 else head
        conv = conv + shifted * w_ref[width - 1 - s:width - s, :]
    _post_conv(conv, q_ref, k_ref, v_ref, key_dim=key_dim, dk=dk)


def _conv_prompt(qkv_in, conv_w, *, key_dim, dk, tt):
    b, t, _ = qkv_in.shape
    width, c = conv_w.shape
    val_dim = c - 2 * key_dim
    sub = tt // 8
    return pl.pallas_call(
        functools.partial(_conv_prompt_kernel, key_dim=key_dim, dk=dk, width=width),
        grid=(b, t // tt),
        in_specs=[pl.BlockSpec((1, tt, c), lambda bi, i: (bi, i, 0)),
                  pl.BlockSpec((1, 8, c), lambda bi, i: (bi, jnp.maximum(i * sub - 1, 0), 0)),
                  pl.BlockSpec((width, c), lambda bi, i: (0, 0))],
        out_specs=[pl.BlockSpec((1, tt, key_dim), lambda bi, i: (bi, i, 0)),
                   pl.BlockSpec((1, tt, key_dim), lambda bi, i: (bi, i, 0)),
                   pl.BlockSpec((1, tt, val_dim), lambda bi, i: (bi, i, 0))],
        out_shape=[jax.ShapeDtypeStruct((b, t, key_dim), F32), jax.ShapeDtypeStruct((b, t, key_dim), F32),
                   jax.ShapeDtypeStruct((b, t, val_dim), F32)],
        compiler_params=_params("parallel", "parallel"),
    )(qkv_in, qkv_in, conv_w)


def _conv_step_kernel(hist_ref, cur_ref, w_ref, q_ref, k_ref, v_ref, *, key_dim, dk, width):
    conv = cur_ref[...] * w_ref[width - 1:width, :]
    for i in range(width - 1):
        conv = conv + hist_ref[i] * w_ref[i:i + 1, :]
    _post_conv(conv, q_ref, k_ref, v_ref, key_dim=key_dim, dk=dk)


def _conv_step(hist_t, cur, conv_w, *, key_dim, dk):
    _, b, c = hist_t.shape
    width = conv_w.shape[0]
    val_dim = c - 2 * key_dim
    return pl.pallas_call(
        functools.partial(_conv_step_kernel, key_dim=key_dim, dk=dk, width=width),
        out_shape=[jax.ShapeDtypeStruct((1, b, key_dim), F32), jax.ShapeDtypeStruct((1, b, key_dim), F32),
                   jax.ShapeDtypeStruct((1, b, val_dim), F32)],
        compiler_params=pltpu.CompilerParams(vmem_limit_bytes=VMEM_LIMIT_BYTES),
    )(hist_t, cur, conv_w)


def _unit_lower_inverse(a, row, col):
    c = a.shape[0]
    shift = INV_BASE.bit_length() - 1
    same = lax.shift_right_logical(row, shift) == lax.shift_right_logical(col, shift)
    a0 = jnp.where(same, a, 0.0)
    x = jnp.where(row == col, 1.0, 0.0) - a0
    power = a0
    for _ in range(shift - 1):
        power = _bdot(power, power)
        x = x + _bdot(x, power)
    size = INV_BASE
    while size < c:
        shift += 1
        wider = lax.shift_right_logical(row, shift) == lax.shift_right_logical(col, shift)
        e = jnp.where(wider & jnp.logical_not(same), a, 0.0)
        x = x - _bdot(x, _bdot(e, x))
        same = wider
        size *= 2
    return x


def _delta_chunk_kernel(q_ref, k_ref, kt_ref, v_ref, beta_ref, g_ref, gt_ref, s0_ref,
                        o_ref, s_ref, state, *, chunk, dv, nh):
    j = pl.program_id(1)
    ci = pl.program_id(2)

    @pl.when(ci == 0)
    def _():
        state[...] = s0_ref[0]

    q = q_ref[0]
    k = k_ref[0]
    kt = kt_ref[0, 0]
    kk = _bdot(k, kt)
    qk = _bdot(q, kt)
    row = lax.broadcasted_iota(jnp.int32, (chunk, chunk), 0)
    col = lax.broadcasted_iota(jnp.int32, (chunk, chunk), 1)
    causal = row >= col
    head_lane = lax.broadcasted_iota(jnp.int32, (1, nh), 1)
    rep = state.shape[0]
    for e in range(rep):
        hv = rep * j + e
        onehot = jnp.where(head_lane == hv, 1.0, 0.0)
        beta = jnp.sum(beta_ref[0] * onehot, axis=1, keepdims=True)
        gcol = jnp.sum(g_ref[0] * onehot, axis=1, keepdims=True)
        grow = gt_ref[0, 0, pl.ds(hv, 1), :]
        decay = jnp.where(causal, jnp.exp(jnp.where(causal, gcol - grow, 0.0)), 0.0)
        a = jnp.where(row > col, beta * kk * decay, 0.0)
        t_inv = _unit_lower_inverse(a, row, col)
        t_off = jnp.where(row == col, 0.0, t_inv)
        eg = jnp.exp(gcol)
        bv = beta * v_ref[0][:, e * dv:(e + 1) * dv]
        bk = (beta * eg) * k
        u_val = bv + _bdot(t_off, bv)
        w = bk + _bdot(t_off, bk)
        s_old = state[e]
        u = u_val - _bdot(w, s_old)
        o_ref[0, :, e * dv:(e + 1) * dv] = _bdot(q * eg, s_old) + _bdot(qk * decay, u)
        g_last = gcol[chunk - 1:chunk, :]
        state[e] = s_old * jnp.exp(g_last) + _bdot(kt * jnp.exp(g_last - grow), u)

    @pl.when(ci == pl.num_programs(2) - 1)
    def _():
        s_ref[0] = state[...]


def _delta_chunked(qn, kn, v, beta, gc, s0, *, dk, dv, chunk):
    b, t, key_dim = qn.shape
    hk = key_dim // dk
    nh = beta.shape[-1]
    rep = nh // hk
    nc = t // chunk
    kt = kn.reshape(b, nc, chunk, key_dim).transpose(0, 1, 3, 2)
    gt = gc.reshape(b, nc, chunk, nh).transpose(0, 1, 3, 2)
    return pl.pallas_call(
        functools.partial(_delta_chunk_kernel, chunk=chunk, dv=dv, nh=nh),
        grid=(b, hk, nc),
        in_specs=[pl.BlockSpec((1, chunk, dk), lambda bi, j, c: (bi, c, j)),
                  pl.BlockSpec((1, chunk, dk), lambda bi, j, c: (bi, c, j)),
                  pl.BlockSpec((1, 1, dk, chunk), lambda bi, j, c: (bi, c, j, 0)),
                  pl.BlockSpec((1, chunk, rep * dv), lambda bi, j, c: (bi, c, j)),
                  pl.BlockSpec((1, chunk, nh), lambda bi, j, c: (bi, c, 0)),
                  pl.BlockSpec((1, chunk, nh), lambda bi, j, c: (bi, c, 0)),
                  pl.BlockSpec((1, 1, nh, chunk), lambda bi, j, c: (bi, c, 0, 0)),
                  pl.BlockSpec((1, rep, dk, dv), lambda bi, j, c: (bi, j, 0, 0))],
        out_specs=[pl.BlockSpec((1, chunk, rep * dv), lambda bi, j, c: (bi, c, j)),
                   pl.BlockSpec((1, rep, dk, dv), lambda bi, j, c: (bi, j, 0, 0))],
        out_shape=[jax.ShapeDtypeStruct((b, t, nh * dv), F32),
                   jax.ShapeDtypeStruct((b, nh, dk, dv), F32)],
        scratch_shapes=[pltpu.VMEM((rep, dk, dv), F32)],
        compiler_params=_params("parallel", "parallel", "arbitrary"),
    )(qn, kn, kt, v, beta, gc, gt, s0)


def _delta_step_kernel(q_ref, k_ref, qc_ref, kc_ref, v_ref, beta_ref, g_ref, s0_ref, o_ref, s_ref,
                       *, dk, dv, rep):
    nh = s0_ref.shape[1]
    for hv in range(nh):
        j = hv // rep
        q_row = q_ref[0][:, j * dk:(j + 1) * dk]
        k_row = k_ref[0][:, j * dk:(j + 1) * dk]
        q_col = qc_ref[0, j * dk:(j + 1) * dk, :]
        k_col = kc_ref[0, j * dk:(j + 1) * dk, :]
        beta = beta_ref[0][:, hv:hv + 1]
        eg = jnp.exp(g_ref[0][:, hv:hv + 1])
        s_old = s0_ref[0, hv]
        k_s = jnp.sum(s_old * k_col, axis=0, keepdims=True)
        q_s = jnp.sum(s_old * q_col, axis=0, keepdims=True)
        u = beta * (v_ref[0][:, hv * dv:(hv + 1) * dv] - eg * k_s)
        qk = jnp.sum(q_row * k_row, axis=-1, keepdims=True)
        o_ref[0, :, hv * dv:(hv + 1) * dv] = eg * q_s + qk * u
        s_ref[0, hv] = s_old * eg + k_col * u


def _delta_step(qn, kn, v, beta, g, s0, *, dk, dv):
    b, _, key_dim = qn.shape
    nh = beta.shape[-1]
    rep = nh // (key_dim // dk)
    row = lambda n: pl.BlockSpec((1, 1, n), lambda bi: (bi, 0, 0))
    col = pl.BlockSpec((1, key_dim, 1), lambda bi: (bi, 0, 0))
    st = pl.BlockSpec((1, nh, dk, dv), lambda bi: (bi, 0, 0, 0))
    return pl.pallas_call(
        functools.partial(_delta_step_kernel, dk=dk, dv=dv, rep=rep), grid=(b,),
        in_specs=[row(key_dim), row(key_dim), col, col, row(nh * dv), row(nh), row(nh), st],
        out_specs=[row(nh * dv), st],
        out_shape=[jax.ShapeDtypeStruct((b, 1, nh * dv), F32), jax.ShapeDtypeStruct(s0.shape, F32)],
        compiler_params=_params("parallel"),
    )(qn, kn, qn.reshape(b, key_dim, 1), kn.reshape(b, key_dim, 1), v, beta, g, s0)


def _gated_out_kernel(o_ref, z_ref, gw_ref, w_ref, r_ref, y_ref, a_scr, *, dv):
    gw = gw_ref[...]
    for hv in range(o_ref.shape[1] // dv):
        sl = slice(hv * dv, (hv + 1) * dv)
        o = o_ref[:, sl]
        inv = lax.rsqrt(jnp.mean(o * o, axis=-1, keepdims=True) + NORM_EPS)
        a_scr[:, sl] = (o * inv * gw * _silu(z_ref[:, sl])).astype(BF16)
    y_ref[...] = r_ref[...] + jnp.dot(a_scr[...], w_ref[...], preferred_element_type=F32)


def _gated_out(o, proj, z_col_block, gnorm_w, w_out_bf, res, *, tm, dv):
    m, val_dim = o.shape
    n = w_out_bf.shape[1]
    return pl.pallas_call(
        functools.partial(_gated_out_kernel, dv=dv), grid=(m // tm,),
        in_specs=[pl.BlockSpec((tm, val_dim), lambda i: (i, 0)),
                  pl.BlockSpec((tm, val_dim), lambda i: (i, z_col_block)),
                  pl.BlockSpec((1, dv), lambda i: (0, 0)),
                  pl.BlockSpec((val_dim, n), lambda i: (0, 0)),
                  pl.BlockSpec((tm, n), lambda i: (i, 0))],
        out_specs=pl.BlockSpec((tm, n), lambda i: (i, 0)),
        out_shape=jax.ShapeDtypeStruct((m, n), F32),
        scratch_shapes=[pltpu.VMEM((tm, val_dim), BF16)],
        compiler_params=_params("parallel"),
    )(o, proj, gnorm_w.reshape(1, dv), w_out_bf, res)


def kernel(x_prompt, x_sample, cache_k, cache_v, page_table, state_conv, state_ssm, norm_mix, w_qkv_attn, w_o_attn, w_in_delta, conv_w_delta, a_log_delta, dt_bias_delta, gnorm_delta, w_out_delta, norm_ffn, w_ffn_in, w_ffn_out, norm_final):
    b, t, d = x_prompt.shape
    sb = x_sample.shape[0]
    assert x_sample.shape[1] == 1, "the sample group advances one token per sequence"
    n_heads, head_dim = cache_k.shape[-2:]
    page = cache_k.shape[2]
    n_pages = page_table.shape[1]
    past_len = n_pages * page
    assert past_len % MOBA_BLOCK == 0 and MOBA_BLOCK % page == 0 and t % MOBA_BLOCK == 0
    depth = norm_mix.shape[0]
    assert depth == 2, "layer 0 is attention, layer 1 is the gated DeltaNet"
    nh_v, dk, dv = state_ssm.shape[2:]
    conv_dim = conv_w_delta.shape[-1]
    val_dim = nh_v * dv
    key_dim = (conv_dim - val_dim) // 2
    assert conv_dim % val_dim == 0 and t % DN_CHUNK == 0

    xp = x_prompt.reshape(b * t, d)
    xs = x_sample.reshape(sb, d)
    tm_p = _row_tile(b * t, 512)

    w_qkv = w_qkv_attn[0].astype(BF16)
    w_o = w_o_attn[0].astype(BF16)
    q_p, k_p, v_p, kb_p, vb_p, km_p = _qkv_rope(
        xp, norm_mix[0], w_qkv, jnp.arange(t, dtype=jnp.int32), tm=MOBA_BLOCK, head_dim=head_dim,
        q_dtype=BF16, with_kmean=True)
    o_p = _moba_prompt(q_p.reshape(b, t, d), kb_p.reshape(b, t, d), vb_p.reshape(b, t, d),
                       km_p.reshape(b, t // MOBA_BLOCK, d), head_dim=head_dim)
    xp = _matmul_res(o_p.reshape(b * t, d), w_o, xp, tm=tm_p)

    q_s, k_s, v_s, _, _ = _qkv_rope(
        xs, norm_mix[0], w_qkv, jnp.full((sb,), past_len, jnp.int32), tm=sb, head_dim=head_dim,
        q_dtype=F32, with_kmean=False)
    pool_k = cache_k[0].reshape(-1, page, d)
    pool_v = cache_v[0].reshape(-1, page, d)
    per_blk = MOBA_BLOCK // page
    nb_past = past_len // MOBA_BLOCK
    ksum = _page_sums(pool_k, page_table).reshape(sb, nb_past, per_blk, d)
    kmean_s = jnp.sum(ksum, axis=2) * (1.0 / MOBA_BLOCK)
    sel = _sample_select(q_s.reshape(sb, 1, d), kmean_s, n_heads)
    logical = sel.transpose(0, 2, 1)[..., None] * per_blk + jnp.arange(per_blk, dtype=jnp.int32)
    phys = jnp.take_along_axis(page_table, logical.reshape(sb, -1), axis=1).reshape(sb, n_heads, -1)
    o_s = _sample_attend(q_s.reshape(sb, 1, d), k_s.reshape(sb, 1, d), v_s.reshape(sb, 1, d),
                         pool_k, pool_v, phys, n_heads=n_heads)
    xs = _matmul_res(o_s.reshape(sb, d), w_o, xs, tm=sb)

    xp = _ffn(xp, norm_ffn[0], w_ffn_in[0].astype(BF16), w_ffn_out[0].astype(BF16), tm=tm_p)
    xs = _ffn(xs, norm_ffn[0], w_ffn_in[0].astype(BF16), w_ffn_out[0].astype(BF16), tm=sb)

    w_in = w_in_delta[0]
    main = conv_dim + val_dim
    w_main = w_in[:, :main].astype(BF16)
    w_ba = jnp.zeros((d, 256), F32).at[:, :nh_v].set(w_in[:, main:main + nh_v])
    w_ba = w_ba.at[:, 128:128 + nh_v].set(w_in[:, main + nh_v:])
    w_out = w_out_delta[0].astype(BF16)
    conv_w = conv_w_delta[0]
    z_block = conv_dim // val_dim

    proj_p = _norm_matmul(xp, norm_mix[1], w_main, tm=tm_p, tn=1024)
    beta_p, _, gc_p = _gates(xp, norm_mix[1], w_ba, a_log_delta[0], dt_bias_delta[0],
                             tm=tm_p, chunk=DN_CHUNK)
    proj_p3 = proj_p.reshape(b, t, main)
    qn_p, kn_p, vv_p = _conv_prompt(proj_p3, conv_w, key_dim=key_dim, dk=dk, tt=256)
    o_dp, ssm_p = _delta_chunked(qn_p, kn_p, vv_p, beta_p.reshape(b, t, nh_v), gc_p.reshape(b, t, nh_v),
                                 jnp.zeros((b,) + state_ssm.shape[2:], F32), dk=dk, dv=dv, chunk=DN_CHUNK)
    xp = _gated_out(o_dp.reshape(b * t, val_dim), proj_p, z_block, gnorm_delta[0], w_out, xp,
                    tm=tm_p, dv=dv)
    conv_p = proj_p3[:, t - (conv_w.shape[0] - 1):, :conv_dim]

    proj_s = _norm_matmul(xs, norm_mix[1], w_main, tm=sb, tn=1024)
    beta_s, g_s, _ = _gates(xs, norm_mix[1], w_ba, a_log_delta[0], dt_bias_delta[0], tm=sb, chunk=1)
    hist_t = state_conv[0].transpose(1, 0, 2)
    cur_s = proj_s[:, :conv_dim]
    qn_s, kn_s, vv_s = _conv_step(hist_t, cur_s, conv_w, key_dim=key_dim, dk=dk)
    o_ds, ssm_s = _delta_step(qn_s.reshape(sb, 1, key_dim), kn_s.reshape(sb, 1, key_dim),
                              vv_s.reshape(sb, 1, val_dim), beta_s.reshape(sb, 1, nh_v),
                              g_s.reshape(sb, 1, nh_v), state_ssm[0], dk=dk, dv=dv)
    xs = _gated_out(o_ds.reshape(sb, val_dim), proj_s, z_block, gnorm_delta[0], w_out, xs, tm=sb, dv=dv)
    conv_s = jnp.concatenate([state_conv[0][:, 1:], cur_s[:, None, :]], axis=1)

    y_p = _ffn(xp, norm_ffn[1], w_ffn_in[1].astype(BF16), w_ffn_out[1].astype(BF16), norm_final, tm=tm_p)
    y_s = _ffn(xs, norm_ffn[1], w_ffn_in[1].astype(BF16), w_ffn_out[1].astype(BF16), norm_final, tm=sb)

    kv_p = (b, t, n_heads, head_dim)
    kv_s = (sb, 1, n_heads, head_dim)
    return (y_p.reshape(b, t, d), y_s.reshape(sb, 1, d),
            k_p.reshape(kv_p)[None], v_p.reshape(kv_p)[None],
            k_s.reshape(kv_s)[None], v_s.reshape(kv_s)[None],
            conv_p[None], ssm_p[None], conv_s[None], ssm_s[None])
```

```python
import functools
import math

import jax
import jax.numpy as jnp
from jax import lax
from jax.experimental import pallas as pl
from jax.experimental.pallas import tpu as pltpu

F32 = jnp.float32
BF16 = jnp.bfloat16

NORM_EPS = 1e-6
ROPE_THETA = 10000.0
MOBA_BLOCK = 256
MOBA_TOPK = 3
DN_CHUNK = 64
KV_GROUP = 4
INV_BASE = 8
VMEM_LIMIT_BYTES = 48 * 1024 * 1024
NEG_INF = float("-inf")
HI = lax.Precision.HIGHEST


def _params(*sem):
    return pltpu.CompilerParams(dimension_semantics=sem, vmem_limit_bytes=VMEM_LIMIT_BYTES)


def _rms(x, w):
    ms = jnp.mean(x * x, axis=-1, keepdims=True)
    return x * lax.rsqrt(ms + NORM_EPS) * w


def _bdot(a, b):
    return jnp.dot(a.astype(BF16), b.astype(BF16), preferred_element_type=F32)


def _bdot_nt(a, b):
    return lax.dot_general(a.astype(BF16), b.astype(BF16), (((1,), (1,)), ((), ())),
                           preferred_element_type=F32)


def _silu(x):
    return x * jax.nn.sigmoid(x)


def _row_tile(m, want):
    return want if m % want == 0 else m


def _qkv_rope_kernel(x_ref, nw_ref, w_ref, cos_ref, sa_ref, sb_ref,
                     q_ref, k_ref, v_ref, kb_ref, vb_ref, *km_ref, d, half, q_scale):
    h = _rms(x_ref[...], nw_ref[...]).astype(BF16)
    rep = d // cos_ref.shape[-1]
    cos = jnp.concatenate([cos_ref[...]] * rep, axis=1)
    sa = jnp.concatenate([sa_ref[...]] * rep, axis=1)
    sb = jnp.concatenate([sb_ref[...]] * rep, axis=1)

    def rope(t):
        return t * cos + pltpu.roll(t, d - half, 1) * sa + pltpu.roll(t, half, 1) * sb

    q = rope(jnp.dot(h, w_ref[:, 0:d], preferred_element_type=F32))
    k = rope(jnp.dot(h, w_ref[:, d:2 * d], preferred_element_type=F32))
    v = jnp.dot(h, w_ref[:, 2 * d:3 * d], preferred_element_type=F32)
    q_ref[...] = (q * q_scale).astype(q_ref.dtype)
    k_ref[...] = k
    v_ref[...] = v
    kb_ref[...] = k.astype(BF16)
    vb_ref[...] = v.astype(BF16)
    if km_ref:
        km_ref[0][0] = jnp.mean(k, axis=0, keepdims=True)


def _rope_tables(pos, head_dim, lanes):
    half = head_dim // 2
    inv_freq = ROPE_THETA ** (-jnp.arange(half, dtype=F32) / half)
    ang = pos.astype(F32)[:, None] * inv_freq[None, :]
    reps = lanes // half
    cos = jnp.tile(jnp.cos(ang), (1, reps))
    sin = jnp.tile(jnp.sin(ang), (1, reps))
    first = (jnp.arange(lanes) % head_dim) < half
    return cos, jnp.where(first, -sin, 0.0), jnp.where(first, 0.0, sin)


def _qkv_rope(x, nw, w_bf, pos, *, tm, head_dim, q_dtype, with_kmean):
    m, d = x.shape
    n_pos = pos.shape[0]
    lanes = 2 * head_dim
    cos, sa, sb = _rope_tables(pos, head_dim, lanes)
    pos_tiles = n_pos // tm
    tab_spec = pl.BlockSpec((tm, lanes), lambda i: (i % pos_tiles, 0))
    row_spec = pl.BlockSpec((tm, d), lambda i: (i, 0))
    out_shape = [jax.ShapeDtypeStruct((m, d), q_dtype), jax.ShapeDtypeStruct((m, d), F32),
                 jax.ShapeDtypeStruct((m, d), F32), jax.ShapeDtypeStruct((m, d), BF16),
                 jax.ShapeDtypeStruct((m, d), BF16)]
    out_specs = [row_spec] * 5
    if with_kmean:
        assert tm == MOBA_BLOCK
        out_shape.append(jax.ShapeDtypeStruct((m // tm, 1, d), F32))
        out_specs.append(pl.BlockSpec((1, 1, d), lambda i: (i, 0, 0)))
    return pl.pallas_call(
        functools.partial(_qkv_rope_kernel, d=d, half=head_dim // 2, q_scale=head_dim ** -0.5),
        grid=(m // tm,),
        in_specs=[row_spec, pl.BlockSpec((1, d), lambda i: (0, 0)),
                  pl.BlockSpec((d, 3 * d), lambda i: (0, 0)), tab_spec, tab_spec, tab_spec],
        out_specs=out_specs, out_shape=out_shape,
        compiler_params=_params("parallel"), name="qkv_rope",
    )(x, nw.reshape(1, d), w_bf, cos, sa, sb)


def _select_blocks(gate, n_past, nb, width):
    blk_id = lax.broadcasted_iota(jnp.int32, (nb, width), 0)
    past = blk_id < n_past
    g = jnp.where(past, gate, NEG_INF)
    sel = jnp.zeros((nb, width), F32)
    for _ in range(min(MOBA_TOPK, nb)):
        top = jnp.max(g, axis=0, keepdims=True)
        idx = jnp.min(jnp.where(g == top, blk_id, nb), axis=0, keepdims=True)
        hit = blk_id == idx
        sel = jnp.where(hit, 1.0, sel)
        g = jnp.where(hit, NEG_INF, g)
    return jnp.where(past, jnp.where(sel > 0.0, 0.0, NEG_INF), NEG_INF)


def _moba_prompt_kernel(q_ref, k_ref, vt_ref, km_ref, o_ref, bias_ref, *, blk, dh):
    qi = pl.program_id(2)
    q = q_ref[0]
    lane = lax.broadcasted_iota(jnp.int32, q.shape, 1)
    nb = km_ref.shape[1]
    kmb = km_ref[0].astype(BF16)
    zero = jnp.zeros_like(q)
    qs = []
    for e in range(2):
        qe = jnp.where((lane >= e * dh) & (lane < (e + 1) * dh), q, zero)
        qs.append(qe)
        bias_ref[e] = _select_blocks(_bdot_nt(kmb, qe), qi, nb, blk)

    kd = k_ref[0, pl.ds(pl.multiple_of(qi * blk, blk), blk), :]
    vtd = vt_ref[0, qi]
    key_i = lax.broadcasted_iota(jnp.int32, (blk, blk), 0)
    qry_i = lax.broadcasted_iota(jnp.int32, (blk, blk), 1)
    sd = [jnp.where(key_i <= qry_i, _bdot_nt(kd, qs[e]), NEG_INF) for e in range(2)]
    md = [jnp.max(s, axis=0, keepdims=True) for s in sd]
    pd = [jnp.exp(s - m) for s, m in zip(sd, md)]
    carry = []
    for e in range(2):
        carry += [md[e], jnp.sum(pd[e], axis=0, keepdims=True),
                  jnp.dot(vtd[e * dh:(e + 1) * dh, :], pd[e].astype(BF16), preferred_element_type=F32)]

    def attend(j0, n, c):
        ks = [k_ref[0, pl.ds(pl.multiple_of((j0 + u) * blk, blk), blk), :] for u in range(n)]
        vts = [vt_ref[0, j0 + u] for u in range(n)]
        ss = [[_bdot_nt(ks[u], qs[e]) + bias_ref[e, pl.ds(j0 + u, 1), :] for u in range(n)]
              for e in range(2)]
        stats, p_alls = [], []
        for e in range(2):
            m, l = c[3 * e], c[3 * e + 1]
            m_new = m
            for s in ss[e]:
                m_new = jnp.maximum(m_new, jnp.max(s, axis=0, keepdims=True))
            alpha = jnp.exp(m - m_new)
            ps = [jnp.exp(s - m_new) for s in ss[e]]
            l = alpha * l
            for p in ps:
                l = l + jnp.sum(p, axis=0, keepdims=True)
            stats.append((m_new, l, alpha))
            p_alls.append(jnp.concatenate([p.astype(BF16) for p in ps], axis=0))
        out = []
        for e in range(2):
            vt_all = jnp.concatenate([vt[e * dh:(e + 1) * dh, :] for vt in vts], axis=1)
            m_new, l, alpha = stats[e]
            out += [m_new, l, alpha * c[3 * e + 2] + jnp.dot(vt_all, p_alls[e],
                                                             preferred_element_type=F32)]
        return tuple(out)

    n_groups = qi // KV_GROUP
    c = lax.fori_loop(0, n_groups, lambda g, c: attend(g * KV_GROUP, KV_GROUP, c), tuple(carry))
    c = lax.fori_loop(n_groups * KV_GROUP, qi, lambda j, c: attend(j, 1, c), c)
    o_t = jnp.concatenate([c[2] / c[1], c[5] / c[4]], axis=0)
    o_ref[0] = o_t.T.astype(o_ref.dtype)


def _moba_prompt(q_bf, k_bf, v_bf, kmean, *, head_dim):
    b, t, d = q_bf.shape
    blk = MOBA_BLOCK
    assert t % blk == 0
    nb = t // blk
    pair = 2 * head_dim
    vt = v_bf.reshape(b, nb, blk, d).transpose(0, 1, 3, 2)
    return pl.pallas_call(
        functools.partial(_moba_prompt_kernel, blk=blk, dh=head_dim),
        grid=(b, d // pair, nb),
        in_specs=[pl.BlockSpec((1, blk, pair), lambda bi, hp, qi: (bi, qi, hp)),
                  pl.BlockSpec((1, t, pair), lambda bi, hp, qi: (bi, 0, hp)),
                  pl.BlockSpec((1, nb, pair, blk), lambda bi, hp, qi: (bi, 0, hp, 0)),
                  pl.BlockSpec((1, nb, pair), lambda bi, hp, qi: (bi, 0, hp))],
        out_specs=pl.BlockSpec((1, blk, pair), lambda bi, hp, qi: (bi, qi, hp)),
        out_shape=jax.ShapeDtypeStruct((b, t, d), BF16),
        scratch_shapes=[pltpu.VMEM((2, nb, blk), F32)],
        compiler_params=_params("parallel", "parallel", "arbitrary"), name="moba_prompt",
    )(q_bf, k_bf, vt, kmean)


PAGES_PER_STEP = 8


def _block_mean_kernel(pt_ref, *refs, per_blk):
    k_refs, o_ref = refs[:-1], refs[-1]
    page = k_refs[0].shape[2]
    for i in range(len(k_refs) // per_blk):
        acc = jnp.sum(k_refs[i * per_blk][0, 0], axis=0)
        for u in range(1, per_blk):
            acc = acc + jnp.sum(k_refs[i * per_blk + u][0, 0], axis=0)
        o_ref[0, i] = acc * (1.0 / (per_blk * page))


def _block_means(cache, layer, page_table, per_blk):
    _, _, page, n_heads, dh = cache.shape
    b, n_pages = page_table.shape
    n_in = PAGES_PER_STEP if n_pages % PAGES_PER_STEP == 0 else per_blk
    assert n_in % per_blk == 0 and n_pages % n_in == 0

    def page_spec(u):
        return pl.BlockSpec((1, 1, page, n_heads, dh),
                            lambda bi, g, pt: (layer, pt[bi * n_pages + g * n_in + u], 0, 0, 0))

    return pl.pallas_call(
        functools.partial(_block_mean_kernel, per_blk=per_blk),
        grid_spec=pltpu.PrefetchScalarGridSpec(
            num_scalar_prefetch=1, grid=(b, n_pages // n_in),
            in_specs=[page_spec(u) for u in range(n_in)],
            out_specs=pl.BlockSpec((1, n_in // per_blk, n_heads, dh), lambda bi, g, pt: (bi, g, 0, 0))),
        out_shape=jax.ShapeDtypeStruct((b, n_pages // per_blk, n_heads, dh), F32),
        compiler_params=_params("parallel", "arbitrary"), name="moba_sample_block_means",
    )(page_table.reshape(-1), *([cache] * n_in))


def _sample_select_kernel(q_ref, ks_ref, seg_ref, idx_ref, *, nb, n_heads):
    prod = ks_ref[0] * q_ref[0]
    gate = jnp.dot(prod, seg_ref[...], precision=HI, preferred_element_type=F32)
    blk_id = lax.broadcasted_iota(jnp.int32, (nb, n_heads), 0)
    rows = []
    for _ in range(min(MOBA_TOPK, nb)):
        top = jnp.max(gate, axis=0, keepdims=True)
        idx = jnp.min(jnp.where(gate == top, blk_id, nb), axis=0, keepdims=True)
        rows.append(idx)
        gate = jnp.where(blk_id == idx, NEG_INF, gate)
    idx_ref[0] = jnp.concatenate(rows, axis=0)


def _sample_select(q, kmean, n_heads):
    b, nb, d = kmean.shape
    topk = min(MOBA_TOPK, nb)
    seg = (jnp.arange(d)[:, None] // (d // n_heads) == jnp.arange(n_heads)[None, :]).astype(F32)
    return pl.pallas_call(
        functools.partial(_sample_select_kernel, nb=nb, n_heads=n_heads),
        grid=(b,),
        in_specs=[pl.BlockSpec((1, 1, d), lambda bi: (bi, 0, 0)),
                  pl.BlockSpec((1, nb, d), lambda bi: (bi, 0, 0)),
                  pl.BlockSpec((d, n_heads), lambda bi: (0, 0))],
        out_specs=pl.BlockSpec((1, topk, n_heads), lambda bi: (bi, 0, 0)),
        out_shape=jax.ShapeDtypeStruct((b, topk, n_heads), jnp.int32),
        compiler_params=_params("parallel"), name="moba_sample_select",
    )(q, kmean, seg)


def _sample_attend_kernel(pg_ref, q_ref, kn_ref, vn_ref, ck_hbm, cv_hbm, o_ref, kbuf, vbuf, sems,
                          *, layer, n_heads, n_sel, dh):
    bi = pl.program_id(0)
    copies = []
    for h in range(n_heads):
        for r in range(n_sel):
            pg = pg_ref[(bi * n_heads + h) * n_sel + r]
            copies.append(pltpu.make_async_copy(ck_hbm.at[layer, pg, :, h, :], kbuf.at[h, r], sems.at[0]))
            copies.append(pltpu.make_async_copy(cv_hbm.at[layer, pg, :, h, :], vbuf.at[h, r], sems.at[1]))
    for c in copies:
        c.start()
    for c in copies:
        c.wait()
    q = q_ref[0]
    page = kbuf.shape[2]
    for h in range(n_heads):
        sl = slice(h * dh, (h + 1) * dh)
        qh = q[:, sl]
        s_self = jnp.sum(qh * kn_ref[0][:, sl], axis=-1, keepdims=True)
        s = jnp.sum(kbuf[h].reshape(n_sel * page, dh) * qh, axis=-1, keepdims=True)
        m = jnp.maximum(jnp.max(s, axis=0, keepdims=True), s_self)
        p = jnp.exp(s - m)
        p_self = jnp.exp(s_self - m)
        l = jnp.sum(p, axis=0, keepdims=True) + p_self
        acc = jnp.sum(p * vbuf[h].reshape(n_sel * page, dh), axis=0, keepdims=True)
        o_ref[0, :, sl] = (acc + p_self * vn_ref[0][:, sl]) / l


def _sample_attend(q, k_new, v_new, cache_k, cache_v, layer, phys):
    b, _, d = q.shape
    _, _, page, n_heads, dh = cache_k.shape
    n_sel = phys.shape[-1]
    row_spec = pl.BlockSpec((1, 1, d), lambda bi, pg: (bi, 0, 0))
    any_spec = pl.BlockSpec(memory_space=pl.ANY)
    return pl.pallas_call(
        functools.partial(_sample_attend_kernel, layer=layer, n_heads=n_heads, n_sel=n_sel, dh=dh),
        grid_spec=pltpu.PrefetchScalarGridSpec(
            num_scalar_prefetch=1, grid=(b,),
            in_specs=[row_spec, row_spec, row_spec, any_spec, any_spec],
            out_specs=row_spec,
            scratch_shapes=[pltpu.VMEM((n_heads, n_sel, page, dh), F32),
                            pltpu.VMEM((n_heads, n_sel, page, dh), F32),
                            pltpu.SemaphoreType.DMA((2,))]),
        out_shape=jax.ShapeDtypeStruct((b, 1, d), F32),
        compiler_params=_params("arbitrary"), name="moba_sample_attend",
    )(phys.reshape(-1), q, k_new, v_new, cache_k, cache_v)


def _matmul_res_kernel(a_ref, w_ref, r_ref, o_ref):
    o_ref[...] = r_ref[...] + _bdot(a_ref[...], w_ref[...])


def _matmul_res(a, w_bf, res, *, tm):
    m, k = a.shape
    n = w_bf.shape[1]
    return pl.pallas_call(
        _matmul_res_kernel, grid=(m // tm,),
        in_specs=[pl.BlockSpec((tm, k), lambda i: (i, 0)), pl.BlockSpec((k, n), lambda i: (0, 0)),
                  pl.BlockSpec((tm, n), lambda i: (i, 0))],
        out_specs=pl.BlockSpec((tm, n), lambda i: (i, 0)),
        out_shape=jax.ShapeDtypeStruct((m, n), F32),
        compiler_params=_params("parallel"), name="matmul_residual",
    )(a, w_bf, res)


def _ffn_kernel(x_ref, nw_ref, wg_ref, wu_ref, wo_ref, *refs, final_norm):
    if final_norm:
        fw_ref, o_ref, h_scr, acc_scr = refs
    else:
        o_ref, h_scr, acc_scr = refs
    j = pl.program_id(1)

    @pl.when(j == 0)
    def _():
        h_scr[...] = _rms(x_ref[...], nw_ref[...]).astype(BF16)
        acc_scr[...] = x_ref[...]

    h = h_scr[...]
    gate = jnp.dot(h, wg_ref[...], preferred_element_type=F32)
    up = jnp.dot(h, wu_ref[...], preferred_element_type=F32)
    acc_scr[...] += _bdot(_silu(gate) * up, wo_ref[...])

    @pl.when(j == pl.num_programs(1) - 1)
    def _():
        y = acc_scr[...]
        o_ref[...] = _rms(y, fw_ref[...]) if final_norm else y


def _ffn(x, nw, w_in_bf, w_out_bf, final_w=None, *, tm, n_hidden_tiles=2):
    m, d = x.shape
    hidden = w_out_bf.shape[0]
    th = hidden // n_hidden_tiles
    assert th * n_hidden_tiles == hidden and th % 128 == 0
    in_specs = [pl.BlockSpec((tm, d), lambda i, j: (i, 0)), pl.BlockSpec((1, d), lambda i, j: (0, 0)),
                pl.BlockSpec((d, th), lambda i, j: (0, j)),
                pl.BlockSpec((d, th), lambda i, j: (0, n_hidden_tiles + j)),
                pl.BlockSpec((th, d), lambda i, j: (j, 0))]
    args = [x, nw.reshape(1, d), w_in_bf, w_in_bf, w_out_bf]
    if final_w is not None:
        in_specs.append(pl.BlockSpec((1, d), lambda i, j: (0, 0)))
        args.append(final_w.reshape(1, d))
    return pl.pallas_call(
        functools.partial(_ffn_kernel, final_norm=final_w is not None),
        grid=(m // tm, n_hidden_tiles), in_specs=in_specs,
        out_specs=pl.BlockSpec((tm, d), lambda i, j: (i, 0)),
        out_shape=jax.ShapeDtypeStruct((m, d), F32),
        scratch_shapes=[pltpu.VMEM((tm, d), BF16), pltpu.VMEM((tm, d), F32)],
        compiler_params=_params("parallel", "arbitrary"), name="swiglu_ffn",
    )(*args)


def _norm_matmul_kernel(x_ref, nw_ref, w_ref, o_ref, h_scr):
    @pl.when(pl.program_id(1) == 0)
    def _():
        h_scr[...] = _rms(x_ref[...], nw_ref[...]).astype(BF16)

    o_ref[...] = jnp.dot(h_scr[...], w_ref[...], preferred_element_type=F32)


def _norm_matmul(x, nw, w_bf, *, tm, tn):
    m, d = x.shape
    n = w_bf.shape[1]
    return pl.pallas_call(
        _norm_matmul_kernel, grid=(m // tm, n // tn),
        in_specs=[pl.BlockSpec((tm, d), lambda i, j: (i, 0)), pl.BlockSpec((1, d), lambda i, j: (0, 0)),
                  pl.BlockSpec((d, tn), lambda i, j: (0, j))],
        out_specs=pl.BlockSpec((tm, tn), lambda i, j: (i, j)),
        out_shape=jax.ShapeDtypeStruct((m, n), F32),
        scratch_shapes=[pltpu.VMEM((tm, d), BF16)],
        compiler_params=_params("parallel", "arbitrary"), name="norm_matmul",
    )(x, nw.reshape(1, d), w_bf)


def _gates_kernel(x_ref, nw_ref, w_ref, alog_ref, dtb_ref, beta_ref, g_ref, gc_ref, *, nh, chunk):
    h = _rms(x_ref[...], nw_ref[...])
    ba = jnp.dot(h, w_ref[...], precision=HI, preferred_element_type=F32)
    b_in, a_in = ba[:, 0:nh], ba[:, 128:128 + nh]
    z = a_in + dtb_ref[...]
    softplus = jnp.maximum(z, 0.0) + jnp.log1p(jnp.exp(-jnp.abs(z)))
    g = -jnp.exp(alog_ref[...]) * softplus
    beta_ref[...] = jax.nn.sigmoid(b_in)
    g_ref[...] = g
    tm = g.shape[0]
    if chunk == 1:
        gc_ref[...] = g
    else:
        r = lax.broadcasted_iota(jnp.int32, (chunk, chunk), 0)
        c = lax.broadcasted_iota(jnp.int32, (chunk, chunk), 1)
        tri = jnp.where(r >= c, 1.0, 0.0)
        for i in range(tm // chunk):
            gc_ref[i * chunk:(i + 1) * chunk, :] = jnp.dot(
                tri, g[i * chunk:(i + 1) * chunk, :], precision=HI, preferred_element_type=F32)


def _gates(x, nw, w_ba, a_log, dt_bias, *, tm, chunk):
    m, d = x.shape
    nh = a_log.shape[0]
    spec = pl.BlockSpec((tm, nh), lambda i: (i, 0))
    small = pl.BlockSpec((1, nh), lambda i: (0, 0))
    return pl.pallas_call(
        functools.partial(_gates_kernel, nh=nh, chunk=chunk), grid=(m // tm,),
        in_specs=[pl.BlockSpec((tm, d), lambda i: (i, 0)), pl.BlockSpec((1, d), lambda i: (0, 0)),
                  pl.BlockSpec((d, 256), lambda i: (0, 0)), small, small],
        out_specs=[spec, spec, spec],
        out_shape=[jax.ShapeDtypeStruct((m, nh), F32)] * 3,
        compiler_params=_params("parallel"), name="delta_gates",
    )(x, nw.reshape(1, d), w_ba, a_log.reshape(1, nh), dt_bias.reshape(1, nh))


def _post_conv(conv, q_ref, k_ref, v_ref, *, key_dim, dk):
    y = _silu(conv)
    for j in range(key_dim // dk):
        for ref, off, scale in ((q_ref, 0, dk ** -0.5), (k_ref, key_dim, 1.0)):
            t = y[:, off + j * dk:off + (j + 1) * dk]
            inv = lax.rsqrt(jnp.sum(t * t, axis=-1, keepdims=True) + NORM_EPS)
            ref[0, :, j * dk:(j + 1) * dk] = t * (inv * scale)
    v_ref[0] = y[:, 2 * key_dim:]


def _conv_prompt_kernel(cur_ref, prev_ref, w_ref, q_ref, k_ref, v_ref, *, key_dim, dk, width):
    cur = cur_ref[0]
    tt = cur.shape[0]
    prev = jnp.where(pl.program_id(1) == 0, 0.0, prev_ref[0])
    row = lax.broadcasted_iota(jnp.int32, (8, cur.shape[1]), 0)
    conv = cur * w_ref[width - 1:width, :]
    for s in range(1, width):
        rolled = pltpu.roll(cur, s, 0)
        head = jnp.where(row < s, pltpu.roll(prev, s, 0), rolled[0:8])
        shifted = jnp.concatenate([head, rolled[8:]], axis=0) if tt > 8 else head
        conv = conv + shifted * w_ref[width - 1 - s:width - s, :]
    _post_conv(conv, q_ref, k_ref, v_ref, key_dim=key_dim, dk=dk)


def _conv_prompt(qkv_in, conv_w, *, key_dim, dk, tt):
    b, t, _ = qkv_in.shape
    width, c = conv_w.shape
    val_dim = c - 2 * key_dim
    sub = tt // 8
    return pl.pallas_call(
        functools.partial(_conv_prompt_kernel, key_dim=key_dim, dk=dk, width=width),
        grid=(b, t // tt),
        in_specs=[pl.BlockSpec((1, tt, c), lambda bi, i: (bi, i, 0)),
                  pl.BlockSpec((1, 8, c), lambda bi, i: (bi, jnp.maximum(i * sub - 1, 0), 0)),
                  pl.BlockSpec((width, c), lambda bi, i: (0, 0))],
        out_specs=[pl.BlockSpec((1, tt, key_dim), lambda bi, i: (bi, i, 0)),
                   pl.BlockSpec((1, tt, key_dim), lambda bi, i: (bi, i, 0)),
                   pl.BlockSpec((1, tt, val_dim), lambda bi, i: (bi, i, 0))],
        out_shape=[jax.ShapeDtypeStruct((b, t, key_dim), F32), jax.ShapeDtypeStruct((b, t, key_dim), F32),
                   jax.ShapeDtypeStruct((b, t, val_dim), F32)],
        compiler_params=_params("parallel", "parallel"), name="delta_conv_prompt",
    )(qkv_in, qkv_in, conv_w)


def _conv_step_kernel(hist_ref, cur_ref, w_ref, q_ref, k_ref, v_ref, *, key_dim, dk, width):
    conv = cur_ref[...] * w_ref[width - 1:width, :]
    for i in range(width - 1):
        conv = conv + hist_ref[i] * w_ref[i:i + 1, :]
    _post_conv(conv, q_ref, k_ref, v_ref, key_dim=key_dim, dk=dk)


def _conv_step(hist_t, cur, conv_w, *, key_dim, dk):
    _, b, c = hist_t.shape
    width = conv_w.shape[0]
    val_dim = c - 2 * key_dim
    return pl.pallas_call(
        functools.partial(_conv_step_kernel, key_dim=key_dim, dk=dk, width=width),
        out_shape=[jax.ShapeDtypeStruct((1, b, key_dim), F32), jax.ShapeDtypeStruct((1, b, key_dim), F32),
                   jax.ShapeDtypeStruct((1, b, val_dim), F32)],
        compiler_params=pltpu.CompilerParams(vmem_limit_bytes=VMEM_LIMIT_BYTES), name="delta_conv_step",
    )(hist_t, cur, conv_w)


def _unit_lower_inverse(mats, row, col):
    c = mats[0].shape[0]
    shift = INV_BASE.bit_length() - 1
    same = lax.shift_right_logical(row, shift) == lax.shift_right_logical(col, shift)
    eye = jnp.where(row == col, 1.0, 0.0)
    power = [jnp.where(same, a, 0.0) for a in mats]
    xs = [eye - p for p in power]
    for _ in range(shift - 1):
        power = [_bdot(p, p) for p in power]
        xs = [x + _bdot(x, p) for x, p in zip(xs, power)]
    size = INV_BASE
    while size < c:
        shift += 1
        wider = lax.shift_right_logical(row, shift) == lax.shift_right_logical(col, shift)
        off = wider & jnp.logical_not(same)
        ex = [_bdot(jnp.where(off, a, 0.0), x) for a, x in zip(mats, xs)]
        xs = [x - _bdot(x, e) for x, e in zip(xs, ex)]
        same = wider
        size *= 2
    return xs


def _delta_chunk_kernel(q_ref, k_ref, kt_ref, v_ref, beta_ref, g_ref, gt_ref, s0_ref,
                        o_ref, s_ref, state, *, chunk, dk, dv):
    ci = pl.program_id(1)

    @pl.when(ci == 0)
    def _():
        state[...] = s0_ref[0]

    row = lax.broadcasted_iota(jnp.int32, (chunk, chunk), 0)
    col = lax.broadcasted_iota(jnp.int32, (chunk, chunk), 1)
    causal = row >= col
    nh = state.shape[0]
    hk = q_ref.shape[2] // dk
    rep = nh // hk
    heads = range(nh)
    beta_all = beta_ref[0]
    g_all = g_ref[0]
    gt_all = gt_ref[0, 0]
    q = [q_ref[0, :, j * dk:(j + 1) * dk] for j in range(hk)]
    k = [k_ref[0, :, j * dk:(j + 1) * dk] for j in range(hk)]
    kt = [kt_ref[0, 0, j * dk:(j + 1) * dk, :] for j in range(hk)]
    kk = [_bdot(k[j], kt[j]) for j in range(hk)]
    qk = [_bdot(q[j], kt[j]) for j in range(hk)]
    beta = [beta_all[:, h:h + 1] for h in heads]
    gcol = [g_all[:, h:h + 1] for h in heads]
    grow = [gt_all[h:h + 1, :] for h in heads]
    decay = [jnp.where(causal, jnp.exp(jnp.where(causal, gcol[h] - grow[h], 0.0)), 0.0)
             for h in heads]
    a = [jnp.where(row > col, beta[h] * kk[h // rep] * decay[h], 0.0) for h in heads]
    t_inv = _unit_lower_inverse(a, row, col)
    t_off = [jnp.where(row == col, 0.0, t) for t in t_inv]
    eg = [jnp.exp(g) for g in gcol]
    bv = [beta[h] * v_ref[0, :, h * dv:(h + 1) * dv] for h in heads]
    bk = [(beta[h] * eg[h]) * k[h // rep] for h in heads]
    u_val = [bv[h] + _bdot(t_off[h], bv[h]) for h in heads]
    w = [bk[h] + _bdot(t_off[h], bk[h]) for h in heads]
    s_old = [state[h] for h in heads]
    u = [u_val[h] - _bdot(w[h], s_old[h]) for h in heads]
    o_state = [_bdot(q[h // rep] * eg[h], s_old[h]) for h in heads]
    o_local = [_bdot(qk[h // rep] * decay[h], u[h]) for h in heads]
    g_last = [g[chunk - 1:chunk, :] for g in gcol]
    s_add = [_bdot(kt[h // rep] * jnp.exp(g_last[h] - grow[h]), u[h]) for h in heads]
    for h in heads:
        o_ref[0, :, h * dv:(h + 1) * dv] = o_state[h] + o_local[h]
        state[h] = s_old[h] * jnp.exp(g_last[h]) + s_add[h]

    @pl.when(ci == pl.num_programs(1) - 1)
    def _():
        s_ref[0] = state[...]


def _delta_chunked(qn, kn, v, beta, gc, s0, *, dk, dv, chunk):
    b, t, key_dim = qn.shape
    nh = beta.shape[-1]
    nc = t // chunk
    kt = kn.reshape(b, nc, chunk, key_dim).transpose(0, 1, 3, 2)
    gt = gc.reshape(b, nc, chunk, nh).transpose(0, 1, 3, 2)
    tok = lambda n: pl.BlockSpec((1, chunk, n), lambda bi, c: (bi, c, 0))
    st = pl.BlockSpec((1, nh, dk, dv), lambda bi, c: (bi, 0, 0, 0))
    return pl.pallas_call(
        functools.partial(_delta_chunk_kernel, chunk=chunk, dk=dk, dv=dv),
        grid=(b, nc),
        in_specs=[tok(key_dim), tok(key_dim),
                  pl.BlockSpec((1, 1, key_dim, chunk), lambda bi, c: (bi, c, 0, 0)),
                  tok(nh * dv), tok(nh), tok(nh),
                  pl.BlockSpec((1, 1, nh, chunk), lambda bi, c: (bi, c, 0, 0)), st],
        out_specs=[tok(nh * dv), st],
        out_shape=[jax.ShapeDtypeStruct((b, t, nh * dv), F32),
                   jax.ShapeDtypeStruct((b, nh, dk, dv), F32)],
        scratch_shapes=[pltpu.VMEM((nh, dk, dv), F32)],
        compiler_params=_params("parallel", "arbitrary"), name="delta_chunked",
    )(qn, kn, kt, v, beta, gc, gt, s0)


def _delta_step_kernel(q_ref, k_ref, qc_ref, kc_ref, v_ref, beta_ref, g_ref, s0_ref, o_ref, s_ref,
                       *, dk, dv, rep):
    nh = s0_ref.shape[1]
    for hv in range(nh):
        j = hv // rep
        q_row = q_ref[0][:, j * dk:(j + 1) * dk]
        k_row = k_ref[0][:, j * dk:(j + 1) * dk]
        q_col = qc_ref[0, j * dk:(j + 1) * dk, :]
        k_col = kc_ref[0, j * dk:(j + 1) * dk, :]
        beta = beta_ref[0][:, hv:hv + 1]
        eg = jnp.exp(g_ref[0][:, hv:hv + 1])
        s_old = s0_ref[0, hv]
        k_s = jnp.sum(s_old * k_col, axis=0, keepdims=True)
        q_s = jnp.sum(s_old * q_col, axis=0, keepdims=True)
        u = beta * (v_ref[0][:, hv * dv:(hv + 1) * dv] - eg * k_s)
        qk = jnp.sum(q_row * k_row, axis=-1, keepdims=True)
        o_ref[0, :, hv * dv:(hv + 1) * dv] = eg * q_s + qk * u
        s_ref[0, hv] = s_old * eg + k_col * u


def _delta_step(qn, kn, v, beta, g, s0, *, dk, dv):
    b, _, key_dim = qn.shape
    nh = beta.shape[-1]
    rep = nh // (key_dim // dk)
    row = lambda n: pl.BlockSpec((1, 1, n), lambda bi: (bi, 0, 0))
    col = pl.BlockSpec((1, key_dim, 1), lambda bi: (bi, 0, 0))
    st = pl.BlockSpec((1, nh, dk, dv), lambda bi: (bi, 0, 0, 0))
    return pl.pallas_call(
        functools.partial(_delta_step_kernel, dk=dk, dv=dv, rep=rep), grid=(b,),
        in_specs=[row(key_dim), row(key_dim), col, col, row(nh * dv), row(nh), row(nh), st],
        out_specs=[row(nh * dv), st],
        out_shape=[jax.ShapeDtypeStruct((b, 1, nh * dv), F32), jax.ShapeDtypeStruct(s0.shape, F32)],
        compiler_params=_params("parallel"), name="delta_step",
    )(qn, kn, qn.reshape(b, key_dim, 1), kn.reshape(b, key_dim, 1), v, beta, g, s0)


def _gated_out_kernel(o_ref, z_ref, gw_ref, w_ref, r_ref, y_ref, a_scr, *, dv):
    gw = gw_ref[...]
    for hv in range(o_ref.shape[1] // dv):
        sl = slice(hv * dv, (hv + 1) * dv)
        o = o_ref[:, sl]
        inv = lax.rsqrt(jnp.mean(o * o, axis=-1, keepdims=True) + NORM_EPS)
        a_scr[:, sl] = (o * inv * gw * _silu(z_ref[:, sl])).astype(BF16)
    y_ref[...] = r_ref[...] + jnp.dot(a_scr[...], w_ref[...], preferred_element_type=F32)


def _gated_out(o, proj, z_col_block, gnorm_w, w_out_bf, res, *, tm, dv):
    m, val_dim = o.shape
    n = w_out_bf.shape[1]
    return pl.pallas_call(
        functools.partial(_gated_out_kernel, dv=dv), grid=(m // tm,),
        in_specs=[pl.BlockSpec((tm, val_dim), lambda i: (i, 0)),
                  pl.BlockSpec((tm, val_dim), lambda i: (i, z_col_block)),
                  pl.BlockSpec((1, dv), lambda i: (0, 0)),
                  pl.BlockSpec((val_dim, n), lambda i: (0, 0)),
                  pl.BlockSpec((tm, n), lambda i: (i, 0))],
        out_specs=pl.BlockSpec((tm, n), lambda i: (i, 0)),
        out_shape=jax.ShapeDtypeStruct((m, n), F32),
        scratch_shapes=[pltpu.VMEM((tm, val_dim), BF16)],
        compiler_params=_params("parallel"), name="delta_gated_out",
    )(o, proj, gnorm_w.reshape(1, dv), w_out_bf, res)


def kernel(x_prompt, x_sample, cache_k, cache_v, page_table, state_conv, state_ssm, norm_mix, w_qkv_attn, w_o_attn, w_in_delta, conv_w_delta, a_log_delta, dt_bias_delta, gnorm_delta, w_out_delta, norm_ffn, w_ffn_in, w_ffn_out, norm_final):
    b, t, d = x_prompt.shape
    sb = x_sample.shape[0]
    assert x_sample.shape[1] == 1, "the sample group advances one token per sequence"
    n_heads, head_dim = cache_k.shape[-2:]
    page = cache_k.shape[2]
    n_pages = page_table.shape[1]
    past_len = n_pages * page
    assert past_len % MOBA_BLOCK == 0 and MOBA_BLOCK % page == 0 and t % MOBA_BLOCK == 0
    depth = norm_mix.shape[0]
    assert depth == 2, "layer 0 is attention, layer 1 is the gated DeltaNet"
    nh_v, dk, dv = state_ssm.shape[2:]
    conv_dim = conv_w_delta.shape[-1]
    val_dim = nh_v * dv
    key_dim = (conv_dim - val_dim) // 2
    assert conv_dim % val_dim == 0 and t % DN_CHUNK == 0

    xp = x_prompt.reshape(b * t, d)
    xs = x_sample.reshape(sb, d)
    tm_p = _row_tile(b * t, 512)

    w_qkv = w_qkv_attn[0].astype(BF16)
    w_o = w_o_attn[0].astype(BF16)
    q_p, k_p, v_p, kb_p, vb_p, km_p = _qkv_rope(
        xp, norm_mix[0], w_qkv, jnp.arange(t, dtype=jnp.int32), tm=MOBA_BLOCK, head_dim=head_dim,
        q_dtype=BF16, with_kmean=True)
    o_p = _moba_prompt(q_p.reshape(b, t, d), kb_p.reshape(b, t, d), vb_p.reshape(b, t, d),
                       km_p.reshape(b, t // MOBA_BLOCK, d), head_dim=head_dim)
    xp = _matmul_res(o_p.reshape(b * t, d), w_o, xp, tm=tm_p)

    q_s, k_s, v_s, _, _ = _qkv_rope(
        xs, norm_mix[0], w_qkv, jnp.full((sb,), past_len, jnp.int32), tm=sb, head_dim=head_dim,
        q_dtype=F32, with_kmean=False)
    per_blk = MOBA_BLOCK // page
    nb_past = past_len // MOBA_BLOCK
    kmean_s = _block_means(cache_k, 0, page_table, per_blk).reshape(sb, nb_past, d)
    sel = _sample_select(q_s.reshape(sb, 1, d), kmean_s, n_heads)
    logical = sel.transpose(0, 2, 1)[..., None] * per_blk + jnp.arange(per_blk, dtype=jnp.int32)
    phys = jnp.take_along_axis(page_table, logical.reshape(sb, -1), axis=1).reshape(sb, n_heads, -1)
    o_s = _sample_attend(q_s.reshape(sb, 1, d), k_s.reshape(sb, 1, d), v_s.reshape(sb, 1, d),
                         cache_k, cache_v, 0, phys)
    xs = _matmul_res(o_s.reshape(sb, d), w_o, xs, tm=sb)

    xp = _ffn(xp, norm_ffn[0], w_ffn_in[0].astype(BF16), w_ffn_out[0].astype(BF16), tm=tm_p)
    xs = _ffn(xs, norm_ffn[0], w_ffn_in[0].astype(BF16), w_ffn_out[0].astype(BF16), tm=sb)

    w_in = w_in_delta[0]
    main = conv_dim + val_dim
    w_main = w_in[:, :main].astype(BF16)
    w_ba = jnp.zeros((d, 256), F32).at[:, :nh_v].set(w_in[:, main:main + nh_v])
    w_ba = w_ba.at[:, 128:128 + nh_v].set(w_in[:, main + nh_v:])
    w_out = w_out_delta[0].astype(BF16)
    conv_w = conv_w_delta[0]
    z_block = conv_dim // val_dim

    proj_p = _norm_matmul(xp, norm_mix[1], w_main, tm=tm_p, tn=1024)
    beta_p, _, gc_p = _gates(xp, norm_mix[1], w_ba, a_log_delta[0], dt_bias_delta[0],
                             tm=tm_p, chunk=DN_CHUNK)
    proj_p3 = proj_p.reshape(b, t, main)
    qn_p, kn_p, vv_p = _conv_prompt(proj_p3, conv_w, key_dim=key_dim, dk=dk, tt=256)
    o_dp, ssm_p = _delta_chunked(qn_p, kn_p, vv_p, beta_p.reshape(b, t, nh_v), gc_p.reshape(b, t, nh_v),
                                 jnp.zeros((b,) + state_ssm.shape[2:], F32), dk=dk, dv=dv, chunk=DN_CHUNK)
    xp = _gated_out(o_dp.reshape(b * t, val_dim), proj_p, z_block, gnorm_delta[0], w_out, xp,
                    tm=tm_p, dv=dv)
    conv_p = proj_p3[:, t - (conv_w.shape[0] - 1):, :conv_dim]

    proj_s = _norm_matmul(xs, norm_mix[1], w_main, tm=sb, tn=1024)
    beta_s, g_s, _ = _gates(xs, norm_mix[1], w_ba, a_log_delta[0], dt_bias_delta[0], tm=sb, chunk=1)
    hist_t = state_conv[0].transpose(1, 0, 2)
    cur_s = proj_s[:, :conv_dim]
    qn_s, kn_s, vv_s = _conv_step(hist_t, cur_s, conv_w, key_dim=key_dim, dk=dk)
    o_ds, ssm_s = _delta_step(qn_s.reshape(sb, 1, key_dim), kn_s.reshape(sb, 1, key_dim),
                              vv_s.reshape(sb, 1, val_dim), beta_s.reshape(sb, 1, nh_v),
                              g_s.reshape(sb, 1, nh_v), state_ssm[0], dk=dk, dv=dv)
    xs = _gated_out(o_ds.reshape(sb, val_dim), proj_s, z_block, gnorm_delta[0], w_out, xs, tm=sb, dv=dv)
    conv_s = jnp.concatenate([state_conv[0][:, 1:], cur_s[:, None, :]], axis=1)

    y_p = _ffn(xp, norm_ffn[1], w_ffn_in[1].astype(BF16), w_ffn_out[1].astype(BF16), norm_final, tm=tm_p)
    y_s = _ffn(xs, norm_ffn[1], w_ffn_in[1].astype(BF16), w_ffn_out[1].astype(BF16), norm_final, tm=sb)

    kv_p = (b, t, n_heads, head_dim)
    kv_s = (sb, 1, n_heads, head_dim)
    return (y_p.reshape(b, t, d), y_s.reshape(sb, 1, d),
            k_p.reshape(kv_p)[None], v_p.reshape(kv_p)[None],
            k_s.reshape(kv_s)[None], v_s.reshape(kv_s)[None],
            conv_p[None], ssm_p[None], conv_s[None], ssm_s[None])
```

```python
import functools
import math

import jax
import jax.numpy as jnp
from jax import lax
from jax.experimental import pallas as pl
from jax.experimental.pallas import tpu as pltpu

F32 = jnp.float32
BF16 = jnp.bfloat16

NORM_EPS = 1e-6
ROPE_THETA = 10000.0
MOBA_BLOCK = 256
MOBA_TOPK = 3
DN_CHUNK = 64
KV_GROUP = 4
INV_BASE = 8
VMEM_LIMIT_BYTES = 48 * 1024 * 1024
NEG_INF = float("-inf")
HI = lax.Precision.HIGHEST


def _params(*sem):
    return pltpu.CompilerParams(dimension_semantics=sem, vmem_limit_bytes=VMEM_LIMIT_BYTES)


def _rms(x, w):
    ms = jnp.mean(x * x, axis=-1, keepdims=True)
    return x * lax.rsqrt(ms + NORM_EPS) * w


def _bdot(a, b):
    return jnp.dot(a.astype(BF16), b.astype(BF16), preferred_element_type=F32)


def _bdot_nt(a, b):
    return lax.dot_general(a.astype(BF16), b.astype(BF16), (((1,), (1,)), ((), ())),
                           preferred_element_type=F32)


def _silu(x):
    return x * jax.nn.sigmoid(x)


def _row_tile(m, want):
    return want if m % want == 0 else m


def _qkv_rope_kernel(x_ref, nw_ref, w_ref, cos_ref, sa_ref, sb_ref,
                     q_ref, k_ref, v_ref, kb_ref, vb_ref, *km_ref, d, half, q_scale):
    h = _rms(x_ref[...], nw_ref[...]).astype(BF16)
    rep = d // cos_ref.shape[-1]
    cos = jnp.concatenate([cos_ref[...]] * rep, axis=1)
    sa = jnp.concatenate([sa_ref[...]] * rep, axis=1)
    sb = jnp.concatenate([sb_ref[...]] * rep, axis=1)

    def rope(t):
        return t * cos + pltpu.roll(t, d - half, 1) * sa + pltpu.roll(t, half, 1) * sb

    q = rope(jnp.dot(h, w_ref[:, 0:d], preferred_element_type=F32))
    k = rope(jnp.dot(h, w_ref[:, d:2 * d], preferred_element_type=F32))
    v = jnp.dot(h, w_ref[:, 2 * d:3 * d], preferred_element_type=F32)
    q_ref[...] = (q * q_scale).astype(q_ref.dtype)
    k_ref[...] = k
    v_ref[...] = v
    kb_ref[...] = k.astype(BF16)
    vb_ref[...] = v.astype(BF16)
    if km_ref:
        km_ref[0][0] = jnp.mean(k, axis=0, keepdims=True)


def _rope_tables(pos, head_dim, lanes):
    half = head_dim // 2
    inv_freq = ROPE_THETA ** (-jnp.arange(half, dtype=F32) / half)
    ang = pos.astype(F32)[:, None] * inv_freq[None, :]
    reps = lanes // half
    cos = jnp.tile(jnp.cos(ang), (1, reps))
    sin = jnp.tile(jnp.sin(ang), (1, reps))
    first = (jnp.arange(lanes) % head_dim) < half
    return cos, jnp.where(first, -sin, 0.0), jnp.where(first, 0.0, sin)


def _qkv_rope(x, nw, w_bf, pos, *, tm, head_dim, q_scale, q_dtype, with_kmean):
    m, d = x.shape
    n_pos = pos.shape[0]
    lanes = 2 * head_dim
    cos, sa, sb = _rope_tables(pos, head_dim, lanes)
    pos_tiles = n_pos // tm
    tab_spec = pl.BlockSpec((tm, lanes), lambda i: (i % pos_tiles, 0))
    row_spec = pl.BlockSpec((tm, d), lambda i: (i, 0))
    out_shape = [jax.ShapeDtypeStruct((m, d), q_dtype), jax.ShapeDtypeStruct((m, d), F32),
                 jax.ShapeDtypeStruct((m, d), F32), jax.ShapeDtypeStruct((m, d), BF16),
                 jax.ShapeDtypeStruct((m, d), BF16)]
    out_specs = [row_spec] * 5
    if with_kmean:
        assert tm == MOBA_BLOCK
        out_shape.append(jax.ShapeDtypeStruct((m // tm, 1, d), F32))
        out_specs.append(pl.BlockSpec((1, 1, d), lambda i: (i, 0, 0)))
    return pl.pallas_call(
        functools.partial(_qkv_rope_kernel, d=d, half=head_dim // 2, q_scale=q_scale),
        grid=(m // tm,),
        in_specs=[row_spec, pl.BlockSpec((1, d), lambda i: (0, 0)),
                  pl.BlockSpec((d, 3 * d), lambda i: (0, 0)), tab_spec, tab_spec, tab_spec],
        out_specs=out_specs, out_shape=out_shape,
        compiler_params=_params("parallel"), name="qkv_rope",
    )(x, nw.reshape(1, d), w_bf, cos, sa, sb)


def _select_blocks(gate, n_past, nb, width):
    blk_id = lax.broadcasted_iota(jnp.int32, (nb, width), 0)
    past = blk_id < n_past
    g = jnp.where(past, gate, NEG_INF)
    sel = jnp.zeros((nb, width), F32)
    for _ in range(min(MOBA_TOPK, nb)):
        top = jnp.max(g, axis=0, keepdims=True)
        idx = jnp.min(jnp.where(g == top, blk_id, nb), axis=0, keepdims=True)
        hit = blk_id == idx
        sel = jnp.where(hit, 1.0, sel)
        g = jnp.where(hit, NEG_INF, g)
    return jnp.where(past, sel, 0.0)


def _moba_prompt_kernel(q_ref, k_ref, vt_ref, km_ref, o_ref, sel_ref, *, blk, dh, group):
    qi = pl.program_id(2)
    q = q_ref[0]
    lane = lax.broadcasted_iota(jnp.int32, q.shape, 1)
    nb = km_ref.shape[1]
    kmb = km_ref[0].astype(BF16)
    zero = jnp.zeros_like(q)
    qs = [jnp.where((lane >= e * dh) & (lane < (e + 1) * dh), q, zero) for e in range(2)]
    gates = [_bdot_nt(kmb, qe) for qe in qs]
    for e in range(2):
        sel_ref[e] = _select_blocks(gates[e], qi, nb, blk)

    kd = k_ref[0, pl.ds(pl.multiple_of(qi * blk, blk), blk), :]
    vtd = vt_ref[0, qi]
    key_i = lax.broadcasted_iota(jnp.int32, (blk, blk), 0)
    qry_i = lax.broadcasted_iota(jnp.int32, (blk, blk), 1)
    sd = [jnp.where(key_i <= qry_i, _bdot_nt(kd, qs[e]), NEG_INF) for e in range(2)]
    md = [jnp.max(s, axis=0, keepdims=True) for s in sd]
    pd = [jnp.exp2(s - m) for s, m in zip(sd, md)]
    carry = []
    for e in range(2):
        carry += [md[e], jnp.sum(pd[e], axis=0, keepdims=True),
                  jnp.dot(vtd[e * dh:(e + 1) * dh, :], pd[e].astype(BF16), preferred_element_type=F32)]

    def attend(g, c):
        j0 = g * group
        ks = [k_ref[0, pl.ds(pl.multiple_of((j0 + u) * blk, blk), blk), :] for u in range(group)]
        vts = [vt_ref[0, j0 + u] for u in range(group)]
        ss = [[_bdot_nt(ks[u], qs[e]) for u in range(group)] for e in range(2)]
        on = [[sel_ref[e, pl.ds(j0 + u, 1), :] > 0.0 for u in range(group)] for e in range(2)]
        stats, ps = [], []
        for e in range(2):
            m, l = c[3 * e], c[3 * e + 1]
            m_new = m
            for u in range(group):
                col_max = jnp.max(ss[e][u], axis=0, keepdims=True)
                m_new = jnp.maximum(m_new, jnp.where(on[e][u], col_max, NEG_INF))
            alpha = jnp.exp2(m - m_new)
            pe = [jnp.exp2(s - m_new) for s in ss[e]]
            l = alpha * l
            for u in range(group):
                l = l + jnp.where(on[e][u], jnp.sum(pe[u], axis=0, keepdims=True), 0.0)
            stats.append((m_new, l, alpha))
            ps.append([p.astype(BF16) for p in pe])
        pv = [[jnp.dot(vts[u][e * dh:(e + 1) * dh, :], ps[e][u], preferred_element_type=F32)
               for u in range(group)] for e in range(2)]
        out = []
        for e in range(2):
            m_new, l, alpha = stats[e]
            acc = alpha * c[3 * e + 2]
            for u in range(group):
                acc = acc + jnp.where(on[e][u], pv[e][u], 0.0)
            out += [m_new, l, acc]
        return tuple(out)

    c = lax.fori_loop(0, (qi + group - 1) // group, attend, tuple(carry))
    o_t = jnp.concatenate([c[2] / c[1], c[5] / c[4]], axis=0)
    o_ref[0] = o_t.T.astype(o_ref.dtype)


def _moba_prompt(q_bf, k_bf, v_bf, kmean, *, head_dim):
    b, t, d = q_bf.shape
    blk = MOBA_BLOCK
    assert t % blk == 0
    nb = t // blk
    pair = 2 * head_dim
    vt = v_bf.reshape(b, nb, blk, d).transpose(0, 1, 3, 2)
    group = KV_GROUP if nb % KV_GROUP == 0 else 1
    return pl.pallas_call(
        functools.partial(_moba_prompt_kernel, blk=blk, dh=head_dim, group=group),
        grid=(b, d // pair, nb),
        in_specs=[pl.BlockSpec((1, blk, pair), lambda bi, hp, qi: (bi, qi, hp)),
                  pl.BlockSpec((1, t, pair), lambda bi, hp, qi: (bi, 0, hp)),
                  pl.BlockSpec((1, nb, pair, blk), lambda bi, hp, qi: (bi, 0, hp, 0)),
                  pl.BlockSpec((1, nb, pair), lambda bi, hp, qi: (bi, 0, hp))],
        out_specs=pl.BlockSpec((1, blk, pair), lambda bi, hp, qi: (bi, qi, hp)),
        out_shape=jax.ShapeDtypeStruct((b, t, d), BF16),
        scratch_shapes=[pltpu.VMEM((2, nb, blk), F32)],
        compiler_params=_params("parallel", "parallel", "arbitrary"), name="moba_prompt",
    )(q_bf, k_bf, vt, kmean)


PAGES_PER_STEP = 8


def _sample_scores_kernel(pt_ref, q_ref, *refs):
    k_refs, o_ref = refs[:-1], refs[-1]
    q = q_ref[0]
    for u, k_ref in enumerate(k_refs):
        o_ref[0, u] = jnp.sum(k_ref[0, 0] * q, axis=1)


def _sample_scores(q_col, cache_t, layer, page_table):
    _, _, n_heads, dh, page = cache_t.shape
    b, n_pages = page_table.shape
    n_in = PAGES_PER_STEP if n_pages % PAGES_PER_STEP == 0 else 1

    def page_spec(u):
        return pl.BlockSpec((1, 1, n_heads, dh, page),
                            lambda bi, g, pt: (layer, pt[bi * n_pages + g * n_in + u], 0, 0, 0))

    return pl.pallas_call(
        _sample_scores_kernel,
        grid_spec=pltpu.PrefetchScalarGridSpec(
            num_scalar_prefetch=1, grid=(b, n_pages // n_in),
            in_specs=[pl.BlockSpec((1, n_heads, dh, 1), lambda bi, g, pt: (bi, 0, 0, 0))]
            + [page_spec(u) for u in range(n_in)],
            out_specs=pl.BlockSpec((1, n_in, n_heads, page), lambda bi, g, pt: (bi, g, 0, 0))),
        out_shape=jax.ShapeDtypeStruct((b, n_pages, n_heads, page), F32),
        compiler_params=_params("parallel", "arbitrary"), name="moba_sample_scores",
    )(page_table.reshape(-1), q_col, *([cache_t] * n_in))


def _sample_select_kernel(s_ref, idx_ref, *, per_blk):
    s = s_ref[0]
    n_pages, n_heads, _ = s.shape
    nb = n_pages // per_blk
    page_sum = jnp.sum(s, axis=-1, keepdims=True)
    gate = jnp.sum(page_sum.reshape(nb, per_blk, n_heads, 1), axis=1)
    blk_id = lax.broadcasted_iota(jnp.int32, gate.shape, 0)
    for r in range(idx_ref.shape[1]):
        top = jnp.max(gate, axis=0, keepdims=True)
        idx = jnp.min(jnp.where(gate == top, blk_id, nb), axis=0, keepdims=True)
        idx_ref[0, r] = idx[0]
        gate = jnp.where(blk_id == idx, NEG_INF, gate)


def _sample_select(scores, per_blk):
    b, n_pages, n_heads, page = scores.shape
    topk = min(MOBA_TOPK, n_pages // per_blk)
    return pl.pallas_call(
        functools.partial(_sample_select_kernel, per_blk=per_blk),
        grid=(b,),
        in_specs=[pl.BlockSpec((1, n_pages, n_heads, page), lambda bi: (bi, 0, 0, 0))],
        out_specs=pl.BlockSpec((1, topk, n_heads, 1), lambda bi: (bi, 0, 0, 0)),
        out_shape=jax.ShapeDtypeStruct((b, topk, n_heads, 1), jnp.int32),
        compiler_params=_params("parallel"), name="moba_sample_select",
    )(scores)


def _sample_attend_kernel(pg_ref, lp_ref, q_ref, kn_ref, vn_ref, s_ref, cv_hbm, o_ref, vbuf, sems,
                          *, layer, n_heads, n_sel, dh):
    bi = pl.program_id(0)
    slot = bi % 2

    def value_copies(seq, buf_slot):
        return [pltpu.make_async_copy(cv_hbm.at[layer, pg_ref[(seq * n_heads + h) * n_sel + r], h],
                                      vbuf.at[buf_slot, h, r], sems.at[buf_slot])
                for h in range(n_heads) for r in range(n_sel)]

    @pl.when(bi == 0)
    def _():
        for c in value_copies(bi, slot):
            c.start()

    @pl.when(bi + 1 < pl.num_programs(0))
    def _():
        for c in value_copies(bi + 1, 1 - slot):
            c.start()

    q = q_ref[0]
    rows, selfs = [], []
    for h in range(n_heads):
        sl = slice(h * dh, (h + 1) * dh)
        rows.append(jnp.concatenate(
            [s_ref[0, lp_ref[(bi * n_heads + h) * n_sel + r], pl.ds(h, 1), :] for r in range(n_sel)],
            axis=1))
        selfs.append(jnp.sum(q[:, sl] * kn_ref[0][:, sl], axis=-1, keepdims=True))
    s = jnp.concatenate(rows, axis=0)
    s_self = jnp.concatenate(selfs, axis=0)
    m = jnp.maximum(jnp.max(s, axis=-1, keepdims=True), s_self)
    p = jnp.exp(s - m)
    p_self = jnp.exp(s_self - m)
    inv_l = 1.0 / (jnp.sum(p, axis=-1, keepdims=True) + p_self)

    for c in value_copies(bi, slot):
        c.wait()
    pv = [_bdot_nt(jnp.broadcast_to(p[h:h + 1, :], (8, p.shape[1])),
                   jnp.concatenate([vbuf[slot, h, r] for r in range(n_sel)], axis=1))[0:1, :]
          for h in range(n_heads)]
    for h in range(n_heads):
        sl = slice(h * dh, (h + 1) * dh)
        o_ref[0, :, sl] = (pv[h] + p_self[h:h + 1, :] * vn_ref[0][:, sl]) * inv_l[h:h + 1, :]


def _sample_attend(q, k_new, v_new, scores, cache_vt, layer, phys, logical):
    b, _, d = q.shape
    _, _, n_heads, dh, page = cache_vt.shape
    n_pages = scores.shape[1]
    n_sel = phys.shape[-1]
    row_spec = pl.BlockSpec((1, 1, d), lambda bi, pg, lp: (bi, 0, 0))
    return pl.pallas_call(
        functools.partial(_sample_attend_kernel, layer=layer, n_heads=n_heads, n_sel=n_sel, dh=dh),
        grid_spec=pltpu.PrefetchScalarGridSpec(
            num_scalar_prefetch=2, grid=(b,),
            in_specs=[row_spec, row_spec, row_spec,
                      pl.BlockSpec((1, n_pages, n_heads, page), lambda bi, pg, lp: (bi, 0, 0, 0)),
                      pl.BlockSpec(memory_space=pl.ANY)],
            out_specs=row_spec,
            scratch_shapes=[pltpu.VMEM((2, n_heads, n_sel, dh, page), F32),
                            pltpu.SemaphoreType.DMA((2,))]),
        out_shape=jax.ShapeDtypeStruct((b, 1, d), F32),
        compiler_params=_params("arbitrary"), name="moba_sample_attend",
    )(phys.reshape(-1), logical.reshape(-1), q, k_new, v_new, scores, cache_vt)


def _matmul_res_kernel(a_ref, w_ref, r_ref, o_ref):
    o_ref[...] = r_ref[...] + _bdot(a_ref[...], w_ref[...])


def _matmul_res(a, w_bf, res, *, tm):
    m, k = a.shape
    n = w_bf.shape[1]
    return pl.pallas_call(
        _matmul_res_kernel, grid=(m // tm,),
        in_specs=[pl.BlockSpec((tm, k), lambda i: (i, 0)), pl.BlockSpec((k, n), lambda i: (0, 0)),
                  pl.BlockSpec((tm, n), lambda i: (i, 0))],
        out_specs=pl.BlockSpec((tm, n), lambda i: (i, 0)),
        out_shape=jax.ShapeDtypeStruct((m, n), F32),
        compiler_params=_params("parallel"), name="matmul_residual",
    )(a, w_bf, res)


def _ffn_kernel(x_ref, nw_ref, wg_ref, wu_ref, wo_ref, *refs, final_norm):
    if final_norm:
        fw_ref, o_ref, h_scr, acc_scr = refs
    else:
        o_ref, h_scr, acc_scr = refs
    j = pl.program_id(1)

    @pl.when(j == 0)
    def _():
        h_scr[...] = _rms(x_ref[...], nw_ref[...]).astype(BF16)
        acc_scr[...] = x_ref[...]

    h = h_scr[...]
    gate = jnp.dot(h, wg_ref[...], preferred_element_type=F32)
    up = jnp.dot(h, wu_ref[...], preferred_element_type=F32)
    acc_scr[...] += _bdot(_silu(gate) * up, wo_ref[...])

    @pl.when(j == pl.num_programs(1) - 1)
    def _():
        y = acc_scr[...]
        o_ref[...] = _rms(y, fw_ref[...]) if final_norm else y


def _ffn(x, nw, w_in_bf, w_out_bf, final_w=None, *, tm, n_hidden_tiles=2):
    m, d = x.shape
    hidden = w_out_bf.shape[0]
    th = hidden // n_hidden_tiles
    assert th * n_hidden_tiles == hidden and th % 128 == 0
    in_specs = [pl.BlockSpec((tm, d), lambda i, j: (i, 0)), pl.BlockSpec((1, d), lambda i, j: (0, 0)),
                pl.BlockSpec((d, th), lambda i, j: (0, j)),
                pl.BlockSpec((d, th), lambda i, j: (0, n_hidden_tiles + j)),
                pl.BlockSpec((th, d), lambda i, j: (j, 0))]
    args = [x, nw.reshape(1, d), w_in_bf, w_in_bf, w_out_bf]
    if final_w is not None:
        in_specs.append(pl.BlockSpec((1, d), lambda i, j: (0, 0)))
        args.append(final_w.reshape(1, d))
    return pl.pallas_call(
        functools.partial(_ffn_kernel, final_norm=final_w is not None),
        grid=(m // tm, n_hidden_tiles), in_specs=in_specs,
        out_specs=pl.BlockSpec((tm, d), lambda i, j: (i, 0)),
        out_shape=jax.ShapeDtypeStruct((m, d), F32),
        scratch_shapes=[pltpu.VMEM((tm, d), BF16), pltpu.VMEM((tm, d), F32)],
        compiler_params=_params("parallel", "arbitrary"), name="swiglu_ffn",
    )(*args)


def _norm_matmul_kernel(x_ref, nw_ref, w_ref, o_ref, h_scr):
    @pl.when(pl.program_id(1) == 0)
    def _():
        h_scr[...] = _rms(x_ref[...], nw_ref[...]).astype(BF16)

    o_ref[...] = jnp.dot(h_scr[...], w_ref[...], preferred_element_type=F32)


def _norm_matmul(x, nw, w_bf, *, tm, tn):
    m, d = x.shape
    n = w_bf.shape[1]
    return pl.pallas_call(
        _norm_matmul_kernel, grid=(m // tm, n // tn),
        in_specs=[pl.BlockSpec((tm, d), lambda i, j: (i, 0)), pl.BlockSpec((1, d), lambda i, j: (0, 0)),
                  pl.BlockSpec((d, tn), lambda i, j: (0, j))],
        out_specs=pl.BlockSpec((tm, tn), lambda i, j: (i, j)),
        out_shape=jax.ShapeDtypeStruct((m, n), F32),
        scratch_shapes=[pltpu.VMEM((tm, d), BF16)],
        compiler_params=_params("parallel", "arbitrary"), name="norm_matmul",
    )(x, nw.reshape(1, d), w_bf)


def _gates_kernel(x_ref, nw_ref, w_ref, alog_ref, dtb_ref, beta_ref, g_ref, gc_ref, *, nh, chunk):
    h = _rms(x_ref[...], nw_ref[...])
    ba = jnp.dot(h, w_ref[...], precision=HI, preferred_element_type=F32)
    b_in, a_in = ba[:, 0:nh], ba[:, 128:128 + nh]
    z = a_in + dtb_ref[...]
    softplus = jnp.maximum(z, 0.0) + jnp.log1p(jnp.exp(-jnp.abs(z)))
    g = -jnp.exp(alog_ref[...]) * softplus
    beta_ref[...] = jax.nn.sigmoid(b_in)
    g_ref[...] = g
    tm = g.shape[0]
    if chunk == 1:
        gc_ref[...] = g
    else:
        r = lax.broadcasted_iota(jnp.int32, (chunk, chunk), 0)
        c = lax.broadcasted_iota(jnp.int32, (chunk, chunk), 1)
        tri = jnp.where(r >= c, 1.0, 0.0)
        for i in range(tm // chunk):
            gc_ref[i * chunk:(i + 1) * chunk, :] = jnp.dot(
                tri, g[i * chunk:(i + 1) * chunk, :], precision=HI, preferred_element_type=F32)


def _gates(x, nw, w_ba, a_log, dt_bias, *, tm, chunk):
    m, d = x.shape
    nh = a_log.shape[0]
    spec = pl.BlockSpec((tm, nh), lambda i: (i, 0))
    small = pl.BlockSpec((1, nh), lambda i: (0, 0))
    return pl.pallas_call(
        functools.partial(_gates_kernel, nh=nh, chunk=chunk), grid=(m // tm,),
        in_specs=[pl.BlockSpec((tm, d), lambda i: (i, 0)), pl.BlockSpec((1, d), lambda i: (0, 0)),
                  pl.BlockSpec((d, 256), lambda i: (0, 0)), small, small],
        out_specs=[spec, spec, spec],
        out_shape=[jax.ShapeDtypeStruct((m, nh), F32)] * 3,
        compiler_params=_params("parallel"), name="delta_gates",
    )(x, nw.reshape(1, d), w_ba, a_log.reshape(1, nh), dt_bias.reshape(1, nh))


def _post_conv(conv, q_ref, k_ref, v_ref, *, key_dim, dk):
    y = _silu(conv)
    for j in range(key_dim // dk):
        for ref, off, scale in ((q_ref, 0, dk ** -0.5), (k_ref, key_dim, 1.0)):
            t = y[:, off + j * dk:off + (j + 1) * dk]
            inv = lax.rsqrt(jnp.sum(t * t, axis=-1, keepdims=True) + NORM_EPS)
            ref[0, :, j * dk:(j + 1) * dk] = t * (inv * scale)
    v_ref[0] = y[:, 2 * key_dim:]


def _conv_prompt_kernel(cur_ref, prev_ref, w_ref, q_ref, k_ref, v_ref, *, key_dim, dk, width):
    cur = cur_ref[0]
    tt = cur.shape[0]
    prev = jnp.where(pl.program_id(1) == 0, 0.0, prev_ref[0])
    row = lax.broadcasted_iota(jnp.int32, (8, cur.shape[1]), 0)
    conv = cur * w_ref[width - 1:width, :]
    for s in range(1, width):
        rolled = pltpu.roll(cur, s, 0)
        head = jnp.where(row < s, pltpu.roll(prev, s, 0), rolled[0:8])
        shifted = jnp.concatenate([head, rolled[8:]], axis=0) if tt > 8 else head
        conv = conv + shifted * w_ref[width - 1 - s:width - s, :]
    _post_conv(conv, q_ref, k_ref, v_ref, key_dim=key_dim, dk=dk)


def _conv_prompt(qkv_in, conv_w, *, key_dim, dk, tt):
    b, t, _ = qkv_in.shape
    width, c = conv_w.shape
    val_dim = c - 2 * key_dim
    sub = tt // 8
    return pl.pallas_call(
        functools.partial(_conv_prompt_kernel, key_dim=key_dim, dk=dk, width=width),
        grid=(b, t // tt),
        in_specs=[pl.BlockSpec((1, tt, c), lambda bi, i: (bi, i, 0)),
                  pl.BlockSpec((1, 8, c), lambda bi, i: (bi, jnp.maximum(i * sub - 1, 0), 0)),
                  pl.BlockSpec((width, c), lambda bi, i: (0, 0))],
        out_specs=[pl.BlockSpec((1, tt, key_dim), lambda bi, i: (bi, i, 0)),
                   pl.BlockSpec((1, tt, key_dim), lambda bi, i: (bi, i, 0)),
                   pl.BlockSpec((1, tt, val_dim), lambda bi, i: (bi, i, 0))],
        out_shape=[jax.ShapeDtypeStruct((b, t, key_dim), F32), jax.ShapeDtypeStruct((b, t, key_dim), F32),
                   jax.ShapeDtypeStruct((b, t, val_dim), F32)],
        compiler_params=_params("parallel", "parallel"), name="delta_conv_prompt",
    )(qkv_in, qkv_in, conv_w)


def _conv_step_kernel(hist_ref, cur_ref, w_ref, q_ref, k_ref, v_ref, *, key_dim, dk, width):
    conv = cur_ref[...] * w_ref[width - 1:width, :]
    for i in range(width - 1):
        conv = conv + hist_ref[i] * w_ref[i:i + 1, :]
    _post_conv(conv, q_ref, k_ref, v_ref, key_dim=key_dim, dk=dk)


def _conv_step(hist_t, cur, conv_w, *, key_dim, dk):
    _, b, c = hist_t.shape
    width = conv_w.shape[0]
    val_dim = c - 2 * key_dim
    return pl.pallas_call(
        functools.partial(_conv_step_kernel, key_dim=key_dim, dk=dk, width=width),
        out_shape=[jax.ShapeDtypeStruct((1, b, key_dim), F32), jax.ShapeDtypeStruct((1, b, key_dim), F32),
                   jax.ShapeDtypeStruct((1, b, val_dim), F32)],
        compiler_params=pltpu.CompilerParams(vmem_limit_bytes=VMEM_LIMIT_BYTES), name="delta_conv_step",
    )(hist_t, cur, conv_w)


def _unit_lower_inverse(mats, row, col):
    c = mats[0].shape[0]
    shift = INV_BASE.bit_length() - 1
    same = lax.shift_right_logical(row, shift) == lax.shift_right_logical(col, shift)
    eye = jnp.where(row == col, 1.0, 0.0)
    power = [jnp.where(same, a, 0.0) for a in mats]
    xs = [eye - p for p in power]
    for _ in range(shift - 1):
        power = [_bdot(p, p) for p in power]
        xs = [x + _bdot(x, p) for x, p in zip(xs, power)]
    size = INV_BASE
    while size < c:
        shift += 1
        wider = lax.shift_right_logical(row, shift) == lax.shift_right_logical(col, shift)
        off = wider & jnp.logical_not(same)
        ex = [_bdot(jnp.where(off, a, 0.0), x) for a, x in zip(mats, xs)]
        xs = [x - _bdot(x, e) for x, e in zip(xs, ex)]
        same = wider
        size *= 2
    return xs


def _delta_chunk_kernel(q_ref, k_ref, kt_ref, v_ref, beta_ref, g_ref, gt_ref, s0_ref,
                        o_ref, s_ref, state, *, chunk, dk, dv):
    ci = pl.program_id(1)

    @pl.when(ci == 0)
    def _():
        state[...] = s0_ref[0]

    row = lax.broadcasted_iota(jnp.int32, (chunk, chunk), 0)
    col = lax.broadcasted_iota(jnp.int32, (chunk, chunk), 1)
    causal = row >= col
    nh = state.shape[0]
    hk = q_ref.shape[2] // dk
    rep = nh // hk
    heads = range(nh)
    beta_all = beta_ref[0]
    g_all = g_ref[0]
    gt_all = gt_ref[0, 0]
    q = [q_ref[0, :, j * dk:(j + 1) * dk] for j in range(hk)]
    k = [k_ref[0, :, j * dk:(j + 1) * dk] for j in range(hk)]
    kt = [kt_ref[0, 0, j * dk:(j + 1) * dk, :] for j in range(hk)]
    kk = [_bdot(k[j], kt[j]) for j in range(hk)]
    qk = [_bdot(q[j], kt[j]) for j in range(hk)]
    beta = [beta_all[:, h:h + 1] for h in heads]
    gcol = [g_all[:, h:h + 1] for h in heads]
    grow = [gt_all[h:h + 1, :] for h in heads]
    decay = [jnp.where(causal, jnp.exp(jnp.where(causal, gcol[h] - grow[h], 0.0)), 0.0)
             for h in heads]
    a = [jnp.where(row > col, beta[h] * kk[h // rep] * decay[h], 0.0) for h in heads]
    t_inv = _unit_lower_inverse(a, row, col)
    t_off = [jnp.where(row == col, 0.0, t) for t in t_inv]
    eg = [jnp.exp(g) for g in gcol]
    bv = [beta[h] * v_ref[0, :, h * dv:(h + 1) * dv] for h in heads]
    bk = [(beta[h] * eg[h]) * k[h // rep] for h in heads]
    u_val = [bv[h] + _bdot(t_off[h], bv[h]) for h in heads]
    w = [bk[h] + _bdot(t_off[h], bk[h]) for h in heads]
    s_old = [state[h] for h in heads]
    u = [u_val[h] - _bdot(w[h], s_old[h]) for h in heads]
    o_state = [_bdot(q[h // rep] * eg[h], s_old[h]) for h in heads]
    o_local = [_bdot(qk[h // rep] * decay[h], u[h]) for h in heads]
    g_last = [g[chunk - 1:chunk, :] for g in gcol]
    s_add = [_bdot(kt[h // rep] * jnp.exp(g_last[h] - grow[h]), u[h]) for h in heads]
    for h in heads:
        o_ref[0, :, h * dv:(h + 1) * dv] = o_state[h] + o_local[h]
        state[h] = s_old[h] * jnp.exp(g_last[h]) + s_add[h]

    @pl.when(ci == pl.num_programs(1) - 1)
    def _():
        s_ref[0] = state[...]


def _delta_chunked(qn, kn, v, beta, gc, s0, *, dk, dv, chunk):
    b, t, key_dim = qn.shape
    nh = beta.shape[-1]
    nc = t // chunk
    kt = kn.reshape(b, nc, chunk, key_dim).transpose(0, 1, 3, 2)
    gt = gc.reshape(b, nc, chunk, nh).transpose(0, 1, 3, 2)
    tok = lambda n: pl.BlockSpec((1, chunk, n), lambda bi, c: (bi, c, 0))
    st = pl.BlockSpec((1, nh, dk, dv), lambda bi, c: (bi, 0, 0, 0))
    return pl.pallas_call(
        functools.partial(_delta_chunk_kernel, chunk=chunk, dk=dk, dv=dv),
        grid=(b, nc),
        in_specs=[tok(key_dim), tok(key_dim),
                  pl.BlockSpec((1, 1, key_dim, chunk), lambda bi, c: (bi, c, 0, 0)),
                  tok(nh * dv), tok(nh), tok(nh),
                  pl.BlockSpec((1, 1, nh, chunk), lambda bi, c: (bi, c, 0, 0)), st],
        out_specs=[tok(nh * dv), st],
        out_shape=[jax.ShapeDtypeStruct((b, t, nh * dv), F32),
                   jax.ShapeDtypeStruct((b, nh, dk, dv), F32)],
        scratch_shapes=[pltpu.VMEM((nh, dk, dv), F32)],
        compiler_params=_params("parallel", "arbitrary"), name="delta_chunked",
    )(qn, kn, kt, v, beta, gc, gt, s0)


def _delta_step_kernel(q_ref, k_ref, qc_ref, kc_ref, v_ref, beta_ref, g_ref, s0_ref, o_ref, s_ref,
                       *, dk, dv, rep):
    nh = s0_ref.shape[1]
    for hv in range(nh):
        j = hv // rep
        q_row = q_ref[0][:, j * dk:(j + 1) * dk]
        k_row = k_ref[0][:, j * dk:(j + 1) * dk]
        q_col = qc_ref[0, j * dk:(j + 1) * dk, :]
        k_col = kc_ref[0, j * dk:(j + 1) * dk, :]
        beta = beta_ref[0][:, hv:hv + 1]
        eg = jnp.exp(g_ref[0][:, hv:hv + 1])
        s_old = s0_ref[0, hv]
        k_s = jnp.sum(s_old * k_col, axis=0, keepdims=True)
        q_s = jnp.sum(s_old * q_col, axis=0, keepdims=True)
        u = beta * (v_ref[0][:, hv * dv:(hv + 1) * dv] - eg * k_s)
        qk = jnp.sum(q_row * k_row, axis=-1, keepdims=True)
        o_ref[0, :, hv * dv:(hv + 1) * dv] = eg * q_s + qk * u
        s_ref[0, hv] = s_old * eg + k_col * u


def _delta_step(qn, kn, v, beta, g, s0, *, dk, dv):
    b, _, key_dim = qn.shape
    nh = beta.shape[-1]
    rep = nh // (key_dim // dk)
    row = lambda n: pl.BlockSpec((1, 1, n), lambda bi: (bi, 0, 0))
    col = pl.BlockSpec((1, key_dim, 1), lambda bi: (bi, 0, 0))
    st = pl.BlockSpec((1, nh, dk, dv), lambda bi: (bi, 0, 0, 0))
    return pl.pallas_call(
        functools.partial(_delta_step_kernel, dk=dk, dv=dv, rep=rep), grid=(b,),
        in_specs=[row(key_dim), row(key_dim), col, col, row(nh * dv), row(nh), row(nh), st],
        out_specs=[row(nh * dv), st],
        out_shape=[jax.ShapeDtypeStruct((b, 1, nh * dv), F32), jax.ShapeDtypeStruct(s0.shape, F32)],
        compiler_params=_params("parallel"), name="delta_step",
    )(qn, kn, qn.reshape(b, key_dim, 1), kn.reshape(b, key_dim, 1), v, beta, g, s0)


def _gated_out_kernel(o_ref, z_ref, gw_ref, w_ref, r_ref, y_ref, a_scr, *, dv):
    gw = gw_ref[...]
    for hv in range(o_ref.shape[1] // dv):
        sl = slice(hv * dv, (hv + 1) * dv)
        o = o_ref[:, sl]
        inv = lax.rsqrt(jnp.mean(o * o, axis=-1, keepdims=True) + NORM_EPS)
        a_scr[:, sl] = (o * inv * gw * _silu(z_ref[:, sl])).astype(BF16)
    y_ref[...] = r_ref[...] + jnp.dot(a_scr[...], w_ref[...], preferred_element_type=F32)


def _gated_out(o, proj, z_col_block, gnorm_w, w_out_bf, res, *, tm, dv):
    m, val_dim = o.shape
    n = w_out_bf.shape[1]
    return pl.pallas_call(
        functools.partial(_gated_out_kernel, dv=dv), grid=(m // tm,),
        in_specs=[pl.BlockSpec((tm, val_dim), lambda i: (i, 0)),
                  pl.BlockSpec((tm, val_dim), lambda i: (i, z_col_block)),
                  pl.BlockSpec((1, dv), lambda i: (0, 0)),
                  pl.BlockSpec((val_dim, n), lambda i: (0, 0)),
                  pl.BlockSpec((tm, n), lambda i: (i, 0))],
        out_specs=pl.BlockSpec((tm, n), lambda i: (i, 0)),
        out_shape=jax.ShapeDtypeStruct((m, n), F32),
        scratch_shapes=[pltpu.VMEM((tm, val_dim), BF16)],
        compiler_params=_params("parallel"), name="delta_gated_out",
    )(o, proj, gnorm_w.reshape(1, dv), w_out_bf, res)


def kernel(x_prompt, x_sample, cache_k, cache_v, page_table, state_conv, state_ssm, norm_mix, w_qkv_attn, w_o_attn, w_in_delta, conv_w_delta, a_log_delta, dt_bias_delta, gnorm_delta, w_out_delta, norm_ffn, w_ffn_in, w_ffn_out, norm_final):
    b, t, d = x_prompt.shape
    sb = x_sample.shape[0]
    assert x_sample.shape[1] == 1, "the sample group advances one token per sequence"
    n_heads, head_dim = cache_k.shape[-2:]
    page = cache_k.shape[2]
    n_pages = page_table.shape[1]
    past_len = n_pages * page
    assert past_len % MOBA_BLOCK == 0 and MOBA_BLOCK % page == 0 and t % MOBA_BLOCK == 0
    depth = norm_mix.shape[0]
    assert depth == 2, "layer 0 is attention, layer 1 is the gated DeltaNet"
    nh_v, dk, dv = state_ssm.shape[2:]
    conv_dim = conv_w_delta.shape[-1]
    val_dim = nh_v * dv
    key_dim = (conv_dim - val_dim) // 2
    assert conv_dim % val_dim == 0 and t % DN_CHUNK == 0

    xp = x_prompt.reshape(b * t, d)
    xs = x_sample.reshape(sb, d)
    tm_p = _row_tile(b * t, 512)

    w_qkv = w_qkv_attn[0].astype(BF16)
    w_o = w_o_attn[0].astype(BF16)
    q_p, k_p, v_p, kb_p, vb_p, km_p = _qkv_rope(
        xp, norm_mix[0], w_qkv, jnp.arange(t, dtype=jnp.int32), tm=MOBA_BLOCK, head_dim=head_dim,
        q_scale=head_dim ** -0.5 * math.log2(math.e), q_dtype=BF16, with_kmean=True)
    o_p = _moba_prompt(q_p.reshape(b, t, d), kb_p.reshape(b, t, d), vb_p.reshape(b, t, d),
                       km_p.reshape(b, t // MOBA_BLOCK, d), head_dim=head_dim)
    xp = _matmul_res(o_p.reshape(b * t, d), w_o, xp, tm=tm_p)

    q_s, k_s, v_s, _, _ = _qkv_rope(
        xs, norm_mix[0], w_qkv, jnp.full((sb,), past_len, jnp.int32), tm=sb, head_dim=head_dim,
        q_scale=head_dim ** -0.5, q_dtype=F32, with_kmean=False)
    per_blk = MOBA_BLOCK // page
    cache_kt = cache_k.transpose(0, 1, 3, 4, 2)
    cache_vt = cache_v.transpose(0, 1, 3, 4, 2)
    scores = _sample_scores(q_s.reshape(sb, n_heads, head_dim, 1), cache_kt, 0, page_table)
    sel = _sample_select(scores, per_blk)[..., 0]
    logical = (sel.transpose(0, 2, 1)[..., None] * per_blk
               + jnp.arange(per_blk, dtype=jnp.int32)).reshape(sb, n_heads, -1)
    phys = jnp.take_along_axis(page_table, logical.reshape(sb, -1), axis=1).reshape(sb, n_heads, -1)
    o_s = _sample_attend(q_s.reshape(sb, 1, d), k_s.reshape(sb, 1, d), v_s.reshape(sb, 1, d),
                         scores, cache_vt, 0, phys, logical)
    xs = _matmul_res(o_s.reshape(sb, d), w_o, xs, tm=sb)

    xp = _ffn(xp, norm_ffn[0], w_ffn_in[0].astype(BF16), w_ffn_out[0].astype(BF16), tm=tm_p)
    xs = _ffn(xs, norm_ffn[0], w_ffn_in[0].astype(BF16), w_ffn_out[0].astype(BF16), tm=sb)

    w_in = w_in_delta[0]
    main = conv_dim + val_dim
    w_main = w_in[:, :main].astype(BF16)
    w_ba = jnp.zeros((d, 256), F32).at[:, :nh_v].set(w_in[:, main:main + nh_v])
    w_ba = w_ba.at[:, 128:128 + nh_v].set(w_in[:, main + nh_v:])
    w_out = w_out_delta[0].astype(BF16)
    conv_w = conv_w_delta[0]
    z_block = conv_dim // val_dim

    proj_p = _norm_matmul(xp, norm_mix[1], w_main, tm=tm_p, tn=1024)
    beta_p, _, gc_p = _gates(xp, norm_mix[1], w_ba, a_log_delta[0], dt_bias_delta[0],
                             tm=tm_p, chunk=DN_CHUNK)
    proj_p3 = proj_p.reshape(b, t, main)
    qn_p, kn_p, vv_p = _conv_prompt(proj_p3, conv_w, key_dim=key_dim, dk=dk, tt=256)
    o_dp, ssm_p = _delta_chunked(qn_p, kn_p, vv_p, beta_p.reshape(b, t, nh_v), gc_p.reshape(b, t, nh_v),
                                 jnp.zeros((b,) + state_ssm.shape[2:], F32), dk=dk, dv=dv, chunk=DN_CHUNK)
    xp = _gated_out(o_dp.reshape(b * t, val_dim), proj_p, z_block, gnorm_delta[0], w_out, xp,
                    tm=tm_p, dv=dv)
    conv_p = proj_p3[:, t - (conv_w.shape[0] - 1):, :conv_dim]

    proj_s = _norm_matmul(xs, norm_mix[1], w_main, tm=sb, tn=1024)
    beta_s, g_s, _ = _gates(xs, norm_mix[1], w_ba, a_log_delta[0], dt_bias_delta[0], tm=sb, chunk=1)
    hist_t = state_conv[0].transpose(1, 0, 2)
    cur_s = proj_s[:, :conv_dim]
    qn_s, kn_s, vv_s = _conv_step(hist_t, cur_s, conv_w, key_dim=key_dim, dk=dk)
    o_ds, ssm_s = _delta_step(qn_s.reshape(sb, 1, key_dim), kn_s.reshape(sb, 1, key_dim),
                              vv_s.reshape(sb, 1, val_dim), beta_s.reshape(sb, 1, nh_v),
                              g_s.reshape(sb, 1, nh_v), state_ssm[0], dk=dk, dv=dv)
    xs = _gated_out(o_ds.reshape(sb, val_dim), proj_s, z_block, gnorm_delta[0], w_out, xs, tm=sb, dv=dv)
    conv_s = jnp.concatenate([state_conv[0][:, 1:], cur_s[:, None, :]], axis=1)

    y_p = _ffn(xp, norm_ffn[1], w_ffn_in[1].astype(BF16), w_ffn_out[1].astype(BF16), norm_final, tm=tm_p)
    y_s = _ffn(xs, norm_ffn[1], w_ffn_in[1].astype(BF16), w_ffn_out[1].astype(BF16), norm_final, tm=sb)

    kv_p = (b, t, n_heads, head_dim)
    kv_s = (sb, 1, n_heads, head_dim)
    return (y_p.reshape(b, t, d), y_s.reshape(sb, 1, d),
            k_p.reshape(kv_p)[None], v_p.reshape(kv_p)[None],
            k_s.reshape(kv_s)[None], v_s.reshape(kv_s)[None],
            conv_p[None], ssm_p[None], conv_s[None], ssm_s[None])
```

```python
import functools
import math

import jax
import jax.numpy as jnp
from jax import lax
from jax.experimental import pallas as pl
from jax.experimental.pallas import tpu as pltpu

F32 = jnp.float32
BF16 = jnp.bfloat16

NORM_EPS = 1e-6
ROPE_THETA = 10000.0
MOBA_BLOCK = 256
MOBA_TOPK = 3
DN_CHUNK = 64
KV_GROUP = 4
ONES_ROWS = 16
INV_BASE = 8
VMEM_LIMIT_BYTES = 48 * 1024 * 1024
NEG_INF = float("-inf")
HI = lax.Precision.HIGHEST


def _params(*sem):
    return pltpu.CompilerParams(dimension_semantics=sem, vmem_limit_bytes=VMEM_LIMIT_BYTES)


def _rms(x, w):
    ms = jnp.mean(x * x, axis=-1, keepdims=True)
    return x * lax.rsqrt(ms + NORM_EPS) * w


def _bdot(a, b):
    return jnp.dot(a.astype(BF16), b.astype(BF16), preferred_element_type=F32)


def _bdot_nt(a, b):
    return lax.dot_general(a.astype(BF16), b.astype(BF16), (((1,), (1,)), ((), ())),
                           preferred_element_type=F32)


def _silu(x):
    return x * jax.nn.sigmoid(x)


def _row_tile(m, want):
    return want if m % want == 0 else m


def _qkv_rope_kernel(x_ref, nw_ref, w_ref, cos_ref, sa_ref, sb_ref,
                     q_ref, k_ref, v_ref, *block_refs, d, half, q_scale):
    h = _rms(x_ref[...], nw_ref[...]).astype(BF16)
    rep = d // cos_ref.shape[-1]
    cos = jnp.concatenate([cos_ref[...]] * rep, axis=1)
    sa = jnp.concatenate([sa_ref[...]] * rep, axis=1)
    sb = jnp.concatenate([sb_ref[...]] * rep, axis=1)

    def rope(t):
        return t * cos + pltpu.roll(t, d - half, 1) * sa + pltpu.roll(t, half, 1) * sb

    q = rope(jnp.dot(h, w_ref[:, 0:d], preferred_element_type=F32))
    k = rope(jnp.dot(h, w_ref[:, d:2 * d], preferred_element_type=F32))
    v = jnp.dot(h, w_ref[:, 2 * d:3 * d], preferred_element_type=F32)
    q_ref[...] = (q * q_scale).astype(q_ref.dtype)
    k_ref[...] = k
    v_ref[...] = v
    if block_refs:
        kb_ref, vt_ref, km_ref = block_refs
        kb_ref[...] = k.astype(BF16)
        vt_ref[0] = v.T.astype(BF16)
        km_ref[0] = jnp.mean(k, axis=0, keepdims=True)


def _rope_tables(pos, head_dim, lanes):
    half = head_dim // 2
    inv_freq = ROPE_THETA ** (-jnp.arange(half, dtype=F32) / half)
    ang = pos.astype(F32)[:, None] * inv_freq[None, :]
    reps = lanes // half
    cos = jnp.tile(jnp.cos(ang), (1, reps))
    sin = jnp.tile(jnp.sin(ang), (1, reps))
    first = (jnp.arange(lanes) % head_dim) < half
    return cos, jnp.where(first, -sin, 0.0), jnp.where(first, 0.0, sin)


def _qkv_rope(x, nw, w_bf, pos, *, tm, head_dim, q_scale, q_dtype, block_outputs):
    m, d = x.shape
    n_pos = pos.shape[0]
    lanes = 2 * head_dim
    cos, sa, sb = _rope_tables(pos, head_dim, lanes)
    pos_tiles = n_pos // tm
    tab_spec = pl.BlockSpec((tm, lanes), lambda i: (i % pos_tiles, 0))
    row_spec = pl.BlockSpec((tm, d), lambda i: (i, 0))
    out_shape = [jax.ShapeDtypeStruct((m, d), q_dtype), jax.ShapeDtypeStruct((m, d), F32),
                 jax.ShapeDtypeStruct((m, d), F32)]
    out_specs = [row_spec] * 3
    if block_outputs:
        assert tm == MOBA_BLOCK
        out_shape += [jax.ShapeDtypeStruct((m, d), BF16), jax.ShapeDtypeStruct((m // tm, d, tm), BF16),
                      jax.ShapeDtypeStruct((m // tm, 1, d), F32)]
        out_specs += [row_spec, pl.BlockSpec((1, d, tm), lambda i: (i, 0, 0)),
                      pl.BlockSpec((1, 1, d), lambda i: (i, 0, 0))]
    return pl.pallas_call(
        functools.partial(_qkv_rope_kernel, d=d, half=head_dim // 2, q_scale=q_scale),
        grid=(m // tm,),
        in_specs=[row_spec, pl.BlockSpec((1, d), lambda i: (0, 0)),
                  pl.BlockSpec((d, 3 * d), lambda i: (0, 0)), tab_spec, tab_spec, tab_spec],
        out_specs=out_specs, out_shape=out_shape,
        compiler_params=_params("parallel"), name="qkv_rope",
    )(x, nw.reshape(1, d), w_bf, cos, sa, sb)


def _select_blocks(gate, n_past, nb, width):
    blk_id = lax.broadcasted_iota(jnp.int32, (nb, width), 0)
    past = blk_id < n_past
    g = jnp.where(past, gate, NEG_INF)
    sel = jnp.zeros((nb, width), F32)
    for _ in range(min(MOBA_TOPK, nb)):
        top = jnp.max(g, axis=0, keepdims=True)
        idx = jnp.min(jnp.where(g == top, blk_id, nb), axis=0, keepdims=True)
        hit = blk_id == idx
        sel = jnp.where(hit, 1.0, sel)
        g = jnp.where(hit, NEG_INF, g)
    return jnp.where(past, sel, 0.0)


def _moba_prompt_kernel(q_ref, k_ref, vt_ref, km_ref, o_ref, sel_ref, p_ref, *, blk, dh, group):
    qi = pl.program_id(2)
    q = q_ref[0]
    lane = lax.broadcasted_iota(jnp.int32, q.shape, 1)
    nb = km_ref.shape[1]
    kmb = km_ref[0].astype(BF16)
    zero = jnp.zeros_like(q)
    qs = [jnp.where((lane >= e * dh) & (lane < (e + 1) * dh), q, zero) for e in range(2)]
    gates = [_bdot_nt(kmb, qe) for qe in qs]
    for e in range(2):
        sel_ref[e] = _select_blocks(gates[e], qi, nb, blk)

    kd = k_ref[0, pl.ds(pl.multiple_of(qi * blk, blk), blk), :]
    vtd = vt_ref[0, qi]
    key_i = lax.broadcasted_iota(jnp.int32, (blk, blk), 0)
    qry_i = lax.broadcasted_iota(jnp.int32, (blk, blk), 1)
    ones_rows = jnp.ones((ONES_ROWS, blk), BF16)

    def values_and_sum(vt, e):
        return jnp.concatenate([vt[e * dh:(e + 1) * dh, :], ones_rows], axis=0)

    sd = [jnp.where(key_i <= qry_i, _bdot_nt(kd, qs[e]), NEG_INF) for e in range(2)]
    md = [jnp.max(s, axis=0, keepdims=True) for s in sd]
    pd = [jnp.exp2((s - m).astype(BF16)) for s, m in zip(sd, md)]
    carry = []
    for e in range(2):
        carry += [md[e], jnp.dot(values_and_sum(vtd, e), pd[e], preferred_element_type=F32)]


    @pl.when(qi == 0)
    def _():
        p_ref[...] = jnp.zeros_like(p_ref)

    def pending_values(g_prev):
        j0 = jnp.maximum(g_prev, 0) * group
        live = jnp.where(g_prev >= 0, 1.0, 0.0)
        vts = [vt_ref[0, j0 + u] for u in range(group)]
        pv = [[jnp.dot(values_and_sum(vts[u], e), p_ref[e, u], preferred_element_type=F32)
               for u in range(group)] for e in range(2)]
        tails = []
        for e in range(2):
            tail = jnp.zeros((dh + ONES_ROWS, blk), F32)
            for u in range(group):
                on = sel_ref[e, pl.ds(j0 + u, 1), :] * live > 0.0
                tail = tail + jnp.where(on, pv[e][u], 0.0)
            tails.append(tail)
        return tails

    def attend(g, c):
        j0 = g * group
        ks = [k_ref[0, pl.ds(pl.multiple_of((j0 + u) * blk, blk), blk), :] for u in range(group)]
        ss = [[_bdot_nt(ks[u], qs[e]) for u in range(group)] for e in range(2)]
        tails = pending_values(g - 1)
        out = []
        for e in range(2):
            m, acc = c[2 * e], c[2 * e + 1]
            m_new = m
            for u in range(group):
                col_max = jnp.max(ss[e][u], axis=0, keepdims=True)
                m_new = jnp.maximum(m_new, jnp.where(sel_ref[e, pl.ds(j0 + u, 1), :] > 0.0, col_max, NEG_INF))
            for u in range(group):
                p_ref[e, u] = jnp.exp2((ss[e][u] - m_new).astype(BF16))
            out += [m_new, jnp.exp2(m - m_new) * (acc + tails[e])]
        return tuple(out)

    n_groups = (qi + group - 1) // group
    c = lax.fori_loop(0, n_groups, attend, tuple(carry))
    tails = pending_values(n_groups - 1)
    outs = []
    for e in range(2):
        acc = c[2 * e + 1] + tails[e]
        outs.append(acc[0:dh, :] / acc[dh:dh + 1, :])
    o_ref[0] = jnp.concatenate(outs, axis=0).T.astype(o_ref.dtype)


def _moba_prompt(q_bf, k_bf, vt, kmean, *, head_dim):
    b, t, d = q_bf.shape
    blk = MOBA_BLOCK
    assert t % blk == 0
    nb = t // blk
    pair = 2 * head_dim
    group = KV_GROUP if nb % KV_GROUP == 0 else 1
    return pl.pallas_call(
        functools.partial(_moba_prompt_kernel, blk=blk, dh=head_dim, group=group),
        grid=(b, d // pair, nb),
        in_specs=[pl.BlockSpec((1, blk, pair), lambda bi, hp, qi: (bi, qi, hp)),
                  pl.BlockSpec((1, t, pair), lambda bi, hp, qi: (bi, 0, hp)),
                  pl.BlockSpec((1, nb, pair, blk), lambda bi, hp, qi: (bi, 0, hp, 0)),
                  pl.BlockSpec((1, nb, pair), lambda bi, hp, qi: (bi, 0, hp))],
        out_specs=pl.BlockSpec((1, blk, pair), lambda bi, hp, qi: (bi, qi, hp)),
        out_shape=jax.ShapeDtypeStruct((b, t, d), BF16),
        scratch_shapes=[pltpu.VMEM((2, nb, blk), F32), pltpu.VMEM((2, group, blk, blk), BF16)],
        compiler_params=_params("parallel", "parallel", "arbitrary"), name="moba_prompt",
    )(q_bf, k_bf, vt, kmean)


PAGES_PER_STEP = 8


def _sample_scores_kernel(pt_ref, q_ref, *refs):
    k_refs, o_ref = refs[:-1], refs[-1]
    q = q_ref[0]
    for u, k_ref in enumerate(k_refs):
        o_ref[0, u] = jnp.sum(k_ref[0, 0] * q, axis=1)


def _sample_scores(q_col, cache_t, layer, page_table):
    _, _, n_heads, dh, page = cache_t.shape
    b, n_pages = page_table.shape
    n_in = PAGES_PER_STEP if n_pages % PAGES_PER_STEP == 0 else 1

    def page_spec(u):
        return pl.BlockSpec((1, 1, n_heads, dh, page),
                            lambda bi, g, pt: (layer, pt[bi * n_pages + g * n_in + u], 0, 0, 0))

    return pl.pallas_call(
        _sample_scores_kernel,
        grid_spec=pltpu.PrefetchScalarGridSpec(
            num_scalar_prefetch=1, grid=(b, n_pages // n_in),
            in_specs=[pl.BlockSpec((1, n_heads, dh, 1), lambda bi, g, pt: (bi, 0, 0, 0))]
            + [page_spec(u) for u in range(n_in)],
            out_specs=pl.BlockSpec((1, n_in, n_heads, page), lambda bi, g, pt: (bi, g, 0, 0))),
        out_shape=jax.ShapeDtypeStruct((b, n_pages, n_heads, page), F32),
        compiler_params=_params("parallel", "arbitrary"), name="moba_sample_scores",
    )(page_table.reshape(-1), q_col, *([cache_t] * n_in))


def _sample_select_kernel(s_ref, idx_ref, *, per_blk):
    s = s_ref[0]
    n_pages, n_heads, _ = s.shape
    nb = n_pages // per_blk
    page_sum = jnp.sum(s, axis=-1, keepdims=True)
    gate = jnp.sum(page_sum.reshape(nb, per_blk, n_heads, 1), axis=1)
    blk_id = lax.broadcasted_iota(jnp.int32, gate.shape, 0)
    for r in range(idx_ref.shape[1]):
        top = jnp.max(gate, axis=0, keepdims=True)
        idx = jnp.min(jnp.where(gate == top, blk_id, nb), axis=0, keepdims=True)
        idx_ref[0, r] = idx[0]
        gate = jnp.where(blk_id == idx, NEG_INF, gate)


def _sample_select(scores, per_blk):
    b, n_pages, n_heads, page = scores.shape
    topk = min(MOBA_TOPK, n_pages // per_blk)
    return pl.pallas_call(
        functools.partial(_sample_select_kernel, per_blk=per_blk),
        grid=(b,),
        in_specs=[pl.BlockSpec((1, n_pages, n_heads, page), lambda bi: (bi, 0, 0, 0))],
        out_specs=pl.BlockSpec((1, topk, n_heads, 1), lambda bi: (bi, 0, 0, 0)),
        out_shape=jax.ShapeDtypeStruct((b, topk, n_heads, 1), jnp.int32),
        compiler_params=_params("parallel"), name="moba_sample_select",
    )(scores)


def _sample_attend_kernel(pg_ref, lp_ref, q_ref, kn_ref, vn_ref, s_ref, cv_hbm, o_ref, vbuf, sems,
                          *, layer, n_heads, n_sel, dh):
    bi = pl.program_id(0)
    slot = bi % 2

    def value_copies(seq, buf_slot):
        return [pltpu.make_async_copy(cv_hbm.at[layer, pg_ref[(seq * n_heads + h) * n_sel + r], h],
                                      vbuf.at[buf_slot, h, r], sems.at[buf_slot])
                for h in range(n_heads) for r in range(n_sel)]

    @pl.when(bi == 0)
    def _():
        for c in value_copies(bi, slot):
            c.start()

    @pl.when(bi + 1 < pl.num_programs(0))
    def _():
        for c in value_copies(bi + 1, 1 - slot):
            c.start()

    q = q_ref[0]
    rows, selfs = [], []
    for h in range(n_heads):
        sl = slice(h * dh, (h + 1) * dh)
        rows.append(jnp.concatenate(
            [s_ref[0, lp_ref[(bi * n_heads + h) * n_sel + r], pl.ds(h, 1), :] for r in range(n_sel)],
            axis=1))
        selfs.append(jnp.sum(q[:, sl] * kn_ref[0][:, sl], axis=-1, keepdims=True))
    s = jnp.concatenate(rows, axis=0)
    s_self = jnp.concatenate(selfs, axis=0)
    m = jnp.maximum(jnp.max(s, axis=-1, keepdims=True), s_self)
    p = jnp.exp(s - m)
    p_self = jnp.exp(s_self - m)
    inv_l = 1.0 / (jnp.sum(p, axis=-1, keepdims=True) + p_self)

    for c in value_copies(bi, slot):
        c.wait()
    pv = [_bdot_nt(jnp.broadcast_to(p[h:h + 1, :], (8, p.shape[1])),
                   jnp.concatenate([vbuf[slot, h, r] for r in range(n_sel)], axis=1))[0:1, :]
          for h in range(n_heads)]
    for h in range(n_heads):
        sl = slice(h * dh, (h + 1) * dh)
        o_ref[0, :, sl] = (pv[h] + p_self[h:h + 1, :] * vn_ref[0][:, sl]) * inv_l[h:h + 1, :]


def _sample_attend(q, k_new, v_new, scores, cache_vt, layer, phys, logical):
    b, _, d = q.shape
    _, _, n_heads, dh, page = cache_vt.shape
    n_pages = scores.shape[1]
    n_sel = phys.shape[-1]
    row_spec = pl.BlockSpec((1, 1, d), lambda bi, pg, lp: (bi, 0, 0))
    return pl.pallas_call(
        functools.partial(_sample_attend_kernel, layer=layer, n_heads=n_heads, n_sel=n_sel, dh=dh),
        grid_spec=pltpu.PrefetchScalarGridSpec(
            num_scalar_prefetch=2, grid=(b,),
            in_specs=[row_spec, row_spec, row_spec,
                      pl.BlockSpec((1, n_pages, n_heads, page), lambda bi, pg, lp: (bi, 0, 0, 0)),
                      pl.BlockSpec(memory_space=pl.ANY)],
            out_specs=row_spec,
            scratch_shapes=[pltpu.VMEM((2, n_heads, n_sel, dh, page), F32),
                            pltpu.SemaphoreType.DMA((2,))]),
        out_shape=jax.ShapeDtypeStruct((b, 1, d), F32),
        compiler_params=_params("arbitrary"), name="moba_sample_attend",
    )(phys.reshape(-1), logical.reshape(-1), q, k_new, v_new, scores, cache_vt)


def _matmul_res_kernel(a_ref, w_ref, r_ref, o_ref):
    o_ref[...] = r_ref[...] + _bdot(a_ref[...], w_ref[...])


def _matmul_res(a, w_bf, res, *, tm):
    m, k = a.shape
    n = w_bf.shape[1]
    return pl.pallas_call(
        _matmul_res_kernel, grid=(m // tm,),
        in_specs=[pl.BlockSpec((tm, k), lambda i: (i, 0)), pl.BlockSpec((k, n), lambda i: (0, 0)),
                  pl.BlockSpec((tm, n), lambda i: (i, 0))],
        out_specs=pl.BlockSpec((tm, n), lambda i: (i, 0)),
        out_shape=jax.ShapeDtypeStruct((m, n), F32),
        compiler_params=_params("parallel"), name="matmul_residual",
    )(a, w_bf, res)


def _ffn_kernel(x_ref, nw_ref, wg_ref, wu_ref, wo_ref, *refs, final_norm):
    if final_norm:
        fw_ref, o_ref, h_scr, acc_scr = refs
    else:
        o_ref, h_scr, acc_scr = refs
    j = pl.program_id(1)

    @pl.when(j == 0)
    def _():
        h_scr[...] = _rms(x_ref[...], nw_ref[...]).astype(BF16)
        acc_scr[...] = x_ref[...]

    h = h_scr[...]
    gate = jnp.dot(h, wg_ref[...], preferred_element_type=F32)
    up = jnp.dot(h, wu_ref[...], preferred_element_type=F32)
    acc_scr[...] += _bdot(_silu(gate) * up, wo_ref[...])

    @pl.when(j == pl.num_programs(1) - 1)
    def _():
        y = acc_scr[...]
        o_ref[...] = _rms(y, fw_ref[...]) if final_norm else y


def _ffn(x, nw, w_in_bf, w_out_bf, final_w=None, *, tm, n_hidden_tiles=2):
    m, d = x.shape
    hidden = w_out_bf.shape[0]
    th = hidden // n_hidden_tiles
    assert th * n_hidden_tiles == hidden and th % 128 == 0
    in_specs = [pl.BlockSpec((tm, d), lambda i, j: (i, 0)), pl.BlockSpec((1, d), lambda i, j: (0, 0)),
                pl.BlockSpec((d, th), lambda i, j: (0, j)),
                pl.BlockSpec((d, th), lambda i, j: (0, n_hidden_tiles + j)),
                pl.BlockSpec((th, d), lambda i, j: (j, 0))]
    args = [x, nw.reshape(1, d), w_in_bf, w_in_bf, w_out_bf]
    if final_w is not None:
        in_specs.append(pl.BlockSpec((1, d), lambda i, j: (0, 0)))
        args.append(final_w.reshape(1, d))
    return pl.pallas_call(
        functools.partial(_ffn_kernel, final_norm=final_w is not None),
        grid=(m // tm, n_hidden_tiles), in_specs=in_specs,
        out_specs=pl.BlockSpec((tm, d), lambda i, j: (i, 0)),
        out_shape=jax.ShapeDtypeStruct((m, d), F32),
        scratch_shapes=[pltpu.VMEM((tm, d), BF16), pltpu.VMEM((tm, d), F32)],
        compiler_params=_params("parallel", "arbitrary"), name="swiglu_ffn",
    )(*args)


def _norm_matmul_kernel(x_ref, nw_ref, w_ref, o_ref, h_scr):
    @pl.when(pl.program_id(1) == 0)
    def _():
        h_scr[...] = _rms(x_ref[...], nw_ref[...]).astype(BF16)

    o_ref[...] = jnp.dot(h_scr[...], w_ref[...], preferred_element_type=F32)


def _norm_matmul(x, nw, w_bf, *, tm, tn):
    m, d = x.shape
    n = w_bf.shape[1]
    return pl.pallas_call(
        _norm_matmul_kernel, grid=(m // tm, n // tn),
        in_specs=[pl.BlockSpec((tm, d), lambda i, j: (i, 0)), pl.BlockSpec((1, d), lambda i, j: (0, 0)),
                  pl.BlockSpec((d, tn), lambda i, j: (0, j))],
        out_specs=pl.BlockSpec((tm, tn), lambda i, j: (i, j)),
        out_shape=jax.ShapeDtypeStruct((m, n), F32),
        scratch_shapes=[pltpu.VMEM((tm, d), BF16)],
        compiler_params=_params("parallel", "arbitrary"), name="norm_matmul",
    )(x, nw.reshape(1, d), w_bf)


def _gates_kernel(x_ref, nw_ref, w_ref, alog_ref, dtb_ref, beta_ref, g_ref, gc_ref, *, nh, chunk):
    h = _rms(x_ref[...], nw_ref[...])
    ba = jnp.dot(h, w_ref[...], precision=HI, preferred_element_type=F32)
    b_in, a_in = ba[:, 0:nh], ba[:, 128:128 + nh]
    z = a_in + dtb_ref[...]
    softplus = jnp.maximum(z, 0.0) + jnp.log1p(jnp.exp(-jnp.abs(z)))
    g = -jnp.exp(alog_ref[...]) * softplus
    beta_ref[...] = jax.nn.sigmoid(b_in)
    g_ref[...] = g
    tm = g.shape[0]
    if chunk == 1:
        gc_ref[...] = g
    else:
        r = lax.broadcasted_iota(jnp.int32, (chunk, chunk), 0)
        c = lax.broadcasted_iota(jnp.int32, (chunk, chunk), 1)
        tri = jnp.where(r >= c, 1.0, 0.0)
        for i in range(tm // chunk):
            gc_ref[i * chunk:(i + 1) * chunk, :] = jnp.dot(
                tri, g[i * chunk:(i + 1) * chunk, :], precision=HI, preferred_element_type=F32)


def _gates(x, nw, w_ba, a_log, dt_bias, *, tm, chunk):
    m, d = x.shape
    nh = a_log.shape[0]
    spec = pl.BlockSpec((tm, nh), lambda i: (i, 0))
    small = pl.BlockSpec((1, nh), lambda i: (0, 0))
    return pl.pallas_call(
        functools.partial(_gates_kernel, nh=nh, chunk=chunk), grid=(m // tm,),
        in_specs=[pl.BlockSpec((tm, d), lambda i: (i, 0)), pl.BlockSpec((1, d), lambda i: (0, 0)),
                  pl.BlockSpec((d, 256), lambda i: (0, 0)), small, small],
        out_specs=[spec, spec, spec],
        out_shape=[jax.ShapeDtypeStruct((m, nh), F32)] * 3,
        compiler_params=_params("parallel"), name="delta_gates",
    )(x, nw.reshape(1, d), w_ba, a_log.reshape(1, nh), dt_bias.reshape(1, nh))


def _post_conv(conv, q_ref, k_ref, v_ref, *, key_dim, dk):
    y = _silu(conv)
    for j in range(key_dim // dk):
        for ref, off, scale in ((q_ref, 0, dk ** -0.5), (k_ref, key_dim, 1.0)):
            t = y[:, off + j * dk:off + (j + 1) * dk]
            inv = lax.rsqrt(jnp.sum(t * t, axis=-1, keepdims=True) + NORM_EPS)
            ref[0, :, j * dk:(j + 1) * dk] = t * (inv * scale)
    v_ref[0] = y[:, 2 * key_dim:]


def _conv_prompt_kernel(cur_ref, prev_ref, w_ref, q_ref, k_ref, v_ref, *, key_dim, dk, width):
    cur = cur_ref[0]
    tt = cur.shape[0]
    prev = jnp.where(pl.program_id(1) == 0, 0.0, prev_ref[0])
    row = lax.broadcasted_iota(jnp.int32, (8, cur.shape[1]), 0)
    conv = cur * w_ref[width - 1:width, :]
    for s in range(1, width):
        rolled = pltpu.roll(cur, s, 0)
        head = jnp.where(row < s, pltpu.roll(prev, s, 0), rolled[0:8])
        shifted = jnp.concatenate([head, rolled[8:]], axis=0) if tt > 8 else head
        conv = conv + shifted * w_ref[width - 1 - s:width - s, :]
    _post_conv(conv, q_ref, k_ref, v_ref, key_dim=key_dim, dk=dk)


def _conv_prompt(qkv_in, conv_w, *, key_dim, dk, tt):
    b, t, _ = qkv_in.shape
    width, c = conv_w.shape
    val_dim = c - 2 * key_dim
    sub = tt // 8
    return pl.pallas_call(
        functools.partial(_conv_prompt_kernel, key_dim=key_dim, dk=dk, width=width),
        grid=(b, t // tt),
        in_specs=[pl.BlockSpec((1, tt, c), lambda bi, i: (bi, i, 0)),
                  pl.BlockSpec((1, 8, c), lambda bi, i: (bi, jnp.maximum(i * sub - 1, 0), 0)),
                  pl.BlockSpec((width, c), lambda bi, i: (0, 0))],
        out_specs=[pl.BlockSpec((1, tt, key_dim), lambda bi, i: (bi, i, 0)),
                   pl.BlockSpec((1, tt, key_dim), lambda bi, i: (bi, i, 0)),
                   pl.BlockSpec((1, tt, val_dim), lambda bi, i: (bi, i, 0))],
        out_shape=[jax.ShapeDtypeStruct((b, t, key_dim), F32), jax.ShapeDtypeStruct((b, t, key_dim), F32),
                   jax.ShapeDtypeStruct((b, t, val_dim), F32)],
        compiler_params=_params("parallel", "parallel"), name="delta_conv_prompt",
    )(qkv_in, qkv_in, conv_w)


def _conv_step_kernel(hist_ref, cur_ref, w_ref, q_ref, k_ref, v_ref, *, key_dim, dk, width):
    conv = cur_ref[...] * w_ref[width - 1:width, :]
    for i in range(width - 1):
        conv = conv + hist_ref[i] * w_ref[i:i + 1, :]
    _post_conv(conv, q_ref, k_ref, v_ref, key_dim=key_dim, dk=dk)


def _conv_step(hist_t, cur, conv_w, *, key_dim, dk):
    _, b, c = hist_t.shape
    width = conv_w.shape[0]
    val_dim = c - 2 * key_dim
    return pl.pallas_call(
        functools.partial(_conv_step_kernel, key_dim=key_dim, dk=dk, width=width),
        out_shape=[jax.ShapeDtypeStruct((1, b, key_dim), F32), jax.ShapeDtypeStruct((1, b, key_dim), F32),
                   jax.ShapeDtypeStruct((1, b, val_dim), F32)],
        compiler_params=pltpu.CompilerParams(vmem_limit_bytes=VMEM_LIMIT_BYTES), name="delta_conv_step",
    )(hist_t, cur, conv_w)


def _unit_lower_inverse(mats, row, col):
    c = mats[0].shape[0]
    shift = INV_BASE.bit_length() - 1
    same = lax.shift_right_logical(row, shift) == lax.shift_right_logical(col, shift)
    eye = jnp.where(row == col, 1.0, 0.0)
    power = [jnp.where(same, a, 0.0) for a in mats]
    xs = [eye - p for p in power]
    for _ in range(shift - 1):
        power = [_bdot(p, p) for p in power]
        xs = [x + _bdot(x, p) for x, p in zip(xs, power)]
    size = INV_BASE
    while size < c:
        shift += 1
        wider = lax.shift_right_logical(row, shift) == lax.shift_right_logical(col, shift)
        off = wider & jnp.logical_not(same)
        ex = [_bdot(jnp.where(off, a, 0.0), x) for a, x in zip(mats, xs)]
        xs = [x - _bdot(x, e) for x, e in zip(xs, ex)]
        same = wider
        size *= 2
    return xs


def _delta_chunk_kernel(q_ref, k_ref, v_ref, beta_ref, g_ref, gt_ref, s0_ref,
                        o_ref, s_ref, state, *, chunk, dk, dv):
    ci = pl.program_id(1)

    @pl.when(ci == 0)
    def _():
        state[...] = s0_ref[...]

    row = lax.broadcasted_iota(jnp.int32, (chunk, chunk), 0)
    col = lax.broadcasted_iota(jnp.int32, (chunk, chunk), 1)
    causal = row >= col
    n_seq, nh = state.shape[:2]
    hk = q_ref.shape[2] // dk
    rep = nh // hk
    chains = [(s, h) for s in range(n_seq) for h in range(nh)]
    groups = [(s, j) for s in range(n_seq) for j in range(hk)]
    q = {(s, j): q_ref[s, :, j * dk:(j + 1) * dk] for s, j in groups}
    k = {(s, j): k_ref[s, :, j * dk:(j + 1) * dk] for s, j in groups}
    kt = {c: k[c].T for c in groups}
    kk = {c: _bdot(k[c], kt[c]) for c in groups}
    qk = {c: _bdot(q[c], kt[c]) for c in groups}
    beta_all = [beta_ref[s] for s in range(n_seq)]
    g_all = [g_ref[s] for s in range(n_seq)]
    gt_all = [gt_ref[s, 0] for s in range(n_seq)]
    beta = {(s, h): beta_all[s][:, h:h + 1] for s, h in chains}
    gcol = {(s, h): g_all[s][:, h:h + 1] for s, h in chains}
    grow = {(s, h): gt_all[s][h:h + 1, :] for s, h in chains}
    decay = {c: jnp.where(causal, jnp.exp(jnp.where(causal, gcol[c] - grow[c], 0.0)), 0.0)
             for c in chains}
    a = [jnp.where(row > col, beta[s, h] * kk[s, h // rep] * decay[s, h], 0.0) for s, h in chains]
    t_inv = _unit_lower_inverse(a, row, col)
    t_off = {c: jnp.where(row == col, 0.0, t) for c, t in zip(chains, t_inv)}
    eg = {c: jnp.exp(gcol[c]) for c in chains}
    bv = {(s, h): beta[s, h] * v_ref[s, :, h * dv:(h + 1) * dv] for s, h in chains}
    bk = {(s, h): (beta[s, h] * eg[s, h]) * k[s, h // rep] for s, h in chains}
    u_val = {c: bv[c] + _bdot(t_off[c], bv[c]) for c in chains}
    w = {c: bk[c] + _bdot(t_off[c], bk[c]) for c in chains}
    s_old = {(s, h): state[s, h] for s, h in chains}
    u = {c: u_val[c] - _bdot(w[c], s_old[c]) for c in chains}
    o_state = {(s, h): _bdot(q[s, h // rep] * eg[s, h], s_old[s, h]) for s, h in chains}
    o_local = {(s, h): _bdot(qk[s, h // rep] * decay[s, h], u[s, h]) for s, h in chains}
    g_last = {c: gcol[c][chunk - 1:chunk, :] for c in chains}
    s_add = {(s, h): _bdot(kt[s, h // rep] * jnp.exp(g_last[s, h] - grow[s, h]), u[s, h])
             for s, h in chains}
    for s, h in chains:
        o_ref[s, :, h * dv:(h + 1) * dv] = o_state[s, h] + o_local[s, h]
        state[s, h] = s_old[s, h] * jnp.exp(g_last[s, h]) + s_add[s, h]

    @pl.when(ci == pl.num_programs(1) - 1)
    def _():
        s_ref[...] = state[...]


DELTA_SEQS_PER_STEP = 2


def _delta_chunked(qn, kn, v, beta, gc, s0, *, dk, dv, chunk):
    b, t, key_dim = qn.shape
    nh = beta.shape[-1]
    nc = t // chunk
    ns = DELTA_SEQS_PER_STEP if b % DELTA_SEQS_PER_STEP == 0 else 1
    gt = gc.reshape(b, nc, chunk, nh).transpose(0, 1, 3, 2)
    tok = lambda n: pl.BlockSpec((ns, chunk, n), lambda bi, c: (bi, c, 0))
    st = pl.BlockSpec((ns, nh, dk, dv), lambda bi, c: (bi, 0, 0, 0))
    return pl.pallas_call(
        functools.partial(_delta_chunk_kernel, chunk=chunk, dk=dk, dv=dv),
        grid=(b // ns, nc),
        in_specs=[tok(key_dim), tok(key_dim), tok(nh * dv), tok(nh), tok(nh),
                  pl.BlockSpec((ns, 1, nh, chunk), lambda bi, c: (bi, c, 0, 0)), st],
        out_specs=[tok(nh * dv), st],
        out_shape=[jax.ShapeDtypeStruct((b, t, nh * dv), F32),
                   jax.ShapeDtypeStruct((b, nh, dk, dv), F32)],
        scratch_shapes=[pltpu.VMEM((ns, nh, dk, dv), F32)],
        compiler_params=_params("parallel", "arbitrary"), name="delta_chunked",
    )(qn, kn, v, beta, gc, gt, s0)


def _delta_step_kernel(q_ref, k_ref, qc_ref, kc_ref, v_ref, beta_ref, g_ref, s0_ref, o_ref, s_ref,
                       *, dk, dv, rep):
    nh = s0_ref.shape[1]
    for hv in range(nh):
        j = hv // rep
        q_row = q_ref[0][:, j * dk:(j + 1) * dk]
        k_row = k_ref[0][:, j * dk:(j + 1) * dk]
        q_col = qc_ref[0, j * dk:(j + 1) * dk, :]
        k_col = kc_ref[0, j * dk:(j + 1) * dk, :]
        beta = beta_ref[0][:, hv:hv + 1]
        eg = jnp.exp(g_ref[0][:, hv:hv + 1])
        s_old = s0_ref[0, hv]
        k_s = jnp.sum(s_old * k_col, axis=0, keepdims=True)
        q_s = jnp.sum(s_old * q_col, axis=0, keepdims=True)
        u = beta * (v_ref[0][:, hv * dv:(hv + 1) * dv] - eg * k_s)
        qk = jnp.sum(q_row * k_row, axis=-1, keepdims=True)
        o_ref[0, :, hv * dv:(hv + 1) * dv] = eg * q_s + qk * u
        s_ref[0, hv] = s_old * eg + k_col * u


def _delta_step(qn, kn, v, beta, g, s0, *, dk, dv):
    b, _, key_dim = qn.shape
    nh = beta.shape[-1]
    rep = nh // (key_dim // dk)
    row = lambda n: pl.BlockSpec((1, 1, n), lambda bi: (bi, 0, 0))
    col = pl.BlockSpec((1, key_dim, 1), lambda bi: (bi, 0, 0))
    st = pl.BlockSpec((1, nh, dk, dv), lambda bi: (bi, 0, 0, 0))
    return pl.pallas_call(
        functools.partial(_delta_step_kernel, dk=dk, dv=dv, rep=rep), grid=(b,),
        in_specs=[row(key_dim), row(key_dim), col, col, row(nh * dv), row(nh), row(nh), st],
        out_specs=[row(nh * dv), st],
        out_shape=[jax.ShapeDtypeStruct((b, 1, nh * dv), F32), jax.ShapeDtypeStruct(s0.shape, F32)],
        compiler_params=_params("parallel"), name="delta_step",
    )(qn, kn, qn.reshape(b, key_dim, 1), kn.reshape(b, key_dim, 1), v, beta, g, s0)


def _gated_out_kernel(o_ref, z_ref, gw_ref, w_ref, r_ref, y_ref, a_scr, *, dv):
    gw = gw_ref[...]
    for hv in range(o_ref.shape[1] // dv):
        sl = slice(hv * dv, (hv + 1) * dv)
        o = o_ref[:, sl]
        inv = lax.rsqrt(jnp.mean(o * o, axis=-1, keepdims=True) + NORM_EPS)
        a_scr[:, sl] = (o * inv * gw * _silu(z_ref[:, sl])).astype(BF16)
    y_ref[...] = r_ref[...] + jnp.dot(a_scr[...], w_ref[...], preferred_element_type=F32)


def _gated_out(o, proj, z_col_block, gnorm_w, w_out_bf, res, *, tm, dv):
    m, val_dim = o.shape
    n = w_out_bf.shape[1]
    return pl.pallas_call(
        functools.partial(_gated_out_kernel, dv=dv), grid=(m // tm,),
        in_specs=[pl.BlockSpec((tm, val_dim), lambda i: (i, 0)),
                  pl.BlockSpec((tm, val_dim), lambda i: (i, z_col_block)),
                  pl.BlockSpec((1, dv), lambda i: (0, 0)),
                  pl.BlockSpec((val_dim, n), lambda i: (0, 0)),
                  pl.BlockSpec((tm, n), lambda i: (i, 0))],
        out_specs=pl.BlockSpec((tm, n), lambda i: (i, 0)),
        out_shape=jax.ShapeDtypeStruct((m, n), F32),
        scratch_shapes=[pltpu.VMEM((tm, val_dim), BF16)],
        compiler_params=_params("parallel"), name="delta_gated_out",
    )(o, proj, gnorm_w.reshape(1, dv), w_out_bf, res)


def kernel(x_prompt, x_sample, cache_k, cache_v, page_table, state_conv, state_ssm, norm_mix, w_qkv_attn, w_o_attn, w_in_delta, conv_w_delta, a_log_delta, dt_bias_delta, gnorm_delta, w_out_delta, norm_ffn, w_ffn_in, w_ffn_out, norm_final):
    b, t, d = x_prompt.shape
    sb = x_sample.shape[0]
    assert x_sample.shape[1] == 1, "the sample group advances one token per sequence"
    n_heads, head_dim = cache_k.shape[-2:]
    page = cache_k.shape[2]
    n_pages = page_table.shape[1]
    past_len = n_pages * page
    assert past_len % MOBA_BLOCK == 0 and MOBA_BLOCK % page == 0 and t % MOBA_BLOCK == 0
    depth = norm_mix.shape[0]
    assert depth == 2, "layer 0 is attention, layer 1 is the gated DeltaNet"
    nh_v, dk, dv = state_ssm.shape[2:]
    conv_dim = conv_w_delta.shape[-1]
    val_dim = nh_v * dv
    key_dim = (conv_dim - val_dim) // 2
    assert conv_dim % val_dim == 0 and t % DN_CHUNK == 0

    xp = x_prompt.reshape(b * t, d)
    xs = x_sample.reshape(sb, d)
    tm_p = _row_tile(b * t, 512)

    w_qkv = w_qkv_attn[0].astype(BF16)
    w_o = w_o_attn[0].astype(BF16)
    nb_p = t // MOBA_BLOCK
    q_p, k_p, v_p, kb_p, vt_p, km_p = _qkv_rope(
        xp, norm_mix[0], w_qkv, jnp.arange(t, dtype=jnp.int32), tm=MOBA_BLOCK, head_dim=head_dim,
        q_scale=head_dim ** -0.5 * math.log2(math.e), q_dtype=BF16, block_outputs=True)
    o_p = _moba_prompt(q_p.reshape(b, t, d), kb_p.reshape(b, t, d), vt_p.reshape(b, nb_p, d, MOBA_BLOCK),
                       km_p.reshape(b, nb_p, d), head_dim=head_dim)
    xp = _matmul_res(o_p.reshape(b * t, d), w_o, xp, tm=tm_p)

    q_s, k_s, v_s = _qkv_rope(
        xs, norm_mix[0], w_qkv, jnp.full((sb,), past_len, jnp.int32), tm=sb, head_dim=head_dim,
        q_scale=head_dim ** -0.5, q_dtype=F32, block_outputs=False)
    per_blk = MOBA_BLOCK // page
    cache_kt = cache_k.transpose(0, 1, 3, 4, 2)
    cache_vt = cache_v.transpose(0, 1, 3, 4, 2)
    scores = _sample_scores(q_s.reshape(sb, n_heads, head_dim, 1), cache_kt, 0, page_table)
    sel = _sample_select(scores, per_blk)[..., 0]
    logical = (sel.transpose(0, 2, 1)[..., None] * per_blk
               + jnp.arange(per_blk, dtype=jnp.int32)).reshape(sb, n_heads, -1)
    phys = jnp.take_along_axis(page_table, logical.reshape(sb, -1), axis=1).reshape(sb, n_heads, -1)
    o_s = _sample_attend(q_s.reshape(sb, 1, d), k_s.reshape(sb, 1, d), v_s.reshape(sb, 1, d),
                         scores, cache_vt, 0, phys, logical)
    xs = _matmul_res(o_s.reshape(sb, d), w_o, xs, tm=sb)

    xp = _ffn(xp, norm_ffn[0], w_ffn_in[0].astype(BF16), w_ffn_out[0].astype(BF16), tm=tm_p)
    xs = _ffn(xs, norm_ffn[0], w_ffn_in[0].astype(BF16), w_ffn_out[0].astype(BF16), tm=sb)

    w_in = w_in_delta[0]
    main = conv_dim + val_dim
    w_main = w_in[:, :main].astype(BF16)
    w_ba = jnp.zeros((d, 256), F32).at[:, :nh_v].set(w_in[:, main:main + nh_v])
    w_ba = w_ba.at[:, 128:128 + nh_v].set(w_in[:, main + nh_v:])
    w_out = w_out_delta[0].astype(BF16)
    conv_w = conv_w_delta[0]
    z_block = conv_dim // val_dim

    proj_p = _norm_matmul(xp, norm_mix[1], w_main, tm=tm_p, tn=1024)
    beta_p, _, gc_p = _gates(xp, norm_mix[1], w_ba, a_log_delta[0], dt_bias_delta[0],
                             tm=tm_p, chunk=DN_CHUNK)
    proj_p3 = proj_p.reshape(b, t, main)
    qn_p, kn_p, vv_p = _conv_prompt(proj_p3, conv_w, key_dim=key_dim, dk=dk, tt=256)
    o_dp, ssm_p = _delta_chunked(qn_p, kn_p, vv_p, beta_p.reshape(b, t, nh_v), gc_p.reshape(b, t, nh_v),
                                 jnp.zeros((b,) + state_ssm.shape[2:], F32), dk=dk, dv=dv, chunk=DN_CHUNK)
    xp = _gated_out(o_dp.reshape(b * t, val_dim), proj_p, z_block, gnorm_delta[0], w_out, xp,
                    tm=tm_p, dv=dv)
    conv_p = proj_p3[:, t - (conv_w.shape[0] - 1):, :conv_dim]

    proj_s = _norm_matmul(xs, norm_mix[1], w_main, tm=sb, tn=1024)
    beta_s, g_s, _ = _gates(xs, norm_mix[1], w_ba, a_log_delta[0], dt_bias_delta[0], tm=sb, chunk=1)
    hist_t = state_conv[0].transpose(1, 0, 2)
    cur_s = proj_s[:, :conv_dim]
    qn_s, kn_s, vv_s = _conv_step(hist_t, cur_s, conv_w, key_dim=key_dim, dk=dk)
    o_ds, ssm_s = _delta_step(qn_s.reshape(sb, 1, key_dim), kn_s.reshape(sb, 1, key_dim),
                              vv_s.reshape(sb, 1, val_dim), beta_s.reshape(sb, 1, nh_v),
                              g_s.reshape(sb, 1, nh_v), state_ssm[0], dk=dk, dv=dv)
    xs = _gated_out(o_ds.reshape(sb, val_dim), proj_s, z_block, gnorm_delta[0], w_out, xs, tm=sb, dv=dv)
    conv_s = jnp.concatenate([state_conv[0][:, 1:], cur_s[:, None, :]], axis=1)

    y_p = _ffn(xp, norm_ffn[1], w_ffn_in[1].astype(BF16), w_ffn_out[1].astype(BF16), norm_final, tm=tm_p)
    y_s = _ffn(xs, norm_ffn[1], w_ffn_in[1].astype(BF16), w_ffn_out[1].astype(BF16), norm_final, tm=sb)

    kv_p = (b, t, n_heads, head_dim)
    kv_s = (sb, 1, n_heads, head_dim)
    return (y_p.reshape(b, t, d), y_s.reshape(sb, 1, d),
            k_p.reshape(kv_p)[None], v_p.reshape(kv_p)[None],
            k_s.reshape(kv_s)[None], v_s.reshape(kv_s)[None],
            conv_p[None], ssm_p[None], conv_s[None], ssm_s[None])
```

```python
import functools
import math

import jax
import jax.numpy as jnp
from jax import lax
from jax.experimental import pallas as pl
from jax.experimental.pallas import tpu as pltpu

F32 = jnp.float32
BF16 = jnp.bfloat16

NORM_EPS = 1e-6
ROPE_THETA = 10000.0
MOBA_BLOCK = 256
MOBA_TOPK = 3
DN_CHUNK = 64
KV_GROUP = 4
ONES_ROWS = 16
MOBA_HEADS_PER_STEP = 2
LANES = 128
INV_BASE = 8
VMEM_LIMIT_BYTES = 48 * 1024 * 1024
NEG_INF = float("-inf")
HI = lax.Precision.HIGHEST


def _params(*sem):
    return pltpu.CompilerParams(dimension_semantics=sem, vmem_limit_bytes=VMEM_LIMIT_BYTES)


def _rms(x, w):
    ms = jnp.mean(x * x, axis=-1, keepdims=True)
    return x * lax.rsqrt(ms + NORM_EPS) * w


def _bdot(a, b):
    return jnp.dot(a.astype(BF16), b.astype(BF16), preferred_element_type=F32)


def _bdot_nt(a, b):
    return lax.dot_general(a.astype(BF16), b.astype(BF16), (((1,), (1,)), ((), ())),
                           preferred_element_type=F32)


def _silu(x):
    return x * jax.nn.sigmoid(x)


def _row_tile(m, want):
    return want if m % want == 0 else m


def _qkv_rope_kernel(x_ref, nw_ref, w_ref, cos_ref, sa_ref, sb_ref,
                     q_ref, k_ref, v_ref, *block_refs, d, half, q_scale):
    h = _rms(x_ref[...], nw_ref[...]).astype(BF16)
    rep = d // cos_ref.shape[-1]
    cos = jnp.concatenate([cos_ref[...]] * rep, axis=1)
    sa = jnp.concatenate([sa_ref[...]] * rep, axis=1)
    sb = jnp.concatenate([sb_ref[...]] * rep, axis=1)

    def rope(t):
        return t * cos + pltpu.roll(t, d - half, 1) * sa + pltpu.roll(t, half, 1) * sb

    q = rope(jnp.dot(h, w_ref[:, 0:d], preferred_element_type=F32))
    k = rope(jnp.dot(h, w_ref[:, d:2 * d], preferred_element_type=F32))
    v = jnp.dot(h, w_ref[:, 2 * d:3 * d], preferred_element_type=F32)
    q_ref[...] = (q * q_scale).astype(q_ref.dtype)
    k_ref[...] = k
    v_ref[...] = v
    if block_refs:
        kb_ref, vt_ref, km_ref = block_refs
        kb_ref[...] = k.astype(BF16)
        vt_ref[0] = v.T.astype(BF16)
        km_ref[0] = jnp.mean(k, axis=0, keepdims=True)


def _rope_tables(pos, head_dim, lanes):
    half = head_dim // 2
    inv_freq = ROPE_THETA ** (-jnp.arange(half, dtype=F32) / half)
    ang = pos.astype(F32)[:, None] * inv_freq[None, :]
    reps = lanes // half
    cos = jnp.tile(jnp.cos(ang), (1, reps))
    sin = jnp.tile(jnp.sin(ang), (1, reps))
    first = (jnp.arange(lanes) % head_dim) < half
    return cos, jnp.where(first, -sin, 0.0), jnp.where(first, 0.0, sin)


def _qkv_rope(x, nw, w_bf, pos, *, tm, head_dim, q_scale, q_dtype, block_outputs):
    m, d = x.shape
    n_pos = pos.shape[0]
    lanes = 2 * head_dim
    cos, sa, sb = _rope_tables(pos, head_dim, lanes)
    pos_tiles = n_pos // tm
    tab_spec = pl.BlockSpec((tm, lanes), lambda i: (i % pos_tiles, 0))
    row_spec = pl.BlockSpec((tm, d), lambda i: (i, 0))
    out_shape = [jax.ShapeDtypeStruct((m, d), q_dtype), jax.ShapeDtypeStruct((m, d), F32),
                 jax.ShapeDtypeStruct((m, d), F32)]
    out_specs = [row_spec] * 3
    if block_outputs:
        assert tm == MOBA_BLOCK
        out_shape += [jax.ShapeDtypeStruct((m, d), BF16), jax.ShapeDtypeStruct((m // tm, d, tm), BF16),
                      jax.ShapeDtypeStruct((m // tm, 1, d), F32)]
        out_specs += [row_spec, pl.BlockSpec((1, d, tm), lambda i: (i, 0, 0)),
                      pl.BlockSpec((1, 1, d), lambda i: (i, 0, 0))]
    return pl.pallas_call(
        functools.partial(_qkv_rope_kernel, d=d, half=head_dim // 2, q_scale=q_scale),
        grid=(m // tm,),
        in_specs=[row_spec, pl.BlockSpec((1, d), lambda i: (0, 0)),
                  pl.BlockSpec((d, 3 * d), lambda i: (0, 0)), tab_spec, tab_spec, tab_spec],
        out_specs=out_specs, out_shape=out_shape,
        compiler_params=_params("parallel"), name="qkv_rope",
    )(x, nw.reshape(1, d), w_bf, cos, sa, sb)


def _select_blocks(gate, n_past, nb, width):
    blk_id = lax.broadcasted_iota(jnp.int32, (nb, width), 0)
    past = blk_id < n_past
    g = jnp.where(past, gate, NEG_INF)
    sel = jnp.zeros((nb, width), F32)
    for _ in range(min(MOBA_TOPK, nb)):
        top = jnp.max(g, axis=0, keepdims=True)
        idx = jnp.min(jnp.where(g == top, blk_id, nb), axis=0, keepdims=True)
        hit = blk_id == idx
        sel = jnp.where(hit, 1.0, sel)
        g = jnp.where(hit, NEG_INF, g)
    return jnp.where(past, sel, 0.0)


def _moba_prompt_kernel(q_ref, k_ref, vt_ref, km_ref, o_ref, sel_ref, p_ref, *, blk, dh, group):
    qi = pl.program_id(2)
    heads = range(q_ref.shape[2] // dh)
    per_tile = LANES // dh
    tile_of = lambda e: slice((e // per_tile) * LANES, (e // per_tile + 1) * LANES)
    nb = km_ref.shape[1]
    lane = lax.broadcasted_iota(jnp.int32, (blk, LANES), 1)
    qs = []
    for e in heads:
        qt = q_ref[0, :, tile_of(e)]
        lo = (e % per_tile) * dh
        qs.append(jnp.where((lane >= lo) & (lane < lo + dh), qt, jnp.zeros_like(qt)))

    def keys(j):
        rows = pl.ds(pl.multiple_of(j * blk, blk), blk)
        return [k_ref[0, rows, tile_of(e)] for e in heads]

    gates = [_bdot_nt(km_ref[0, :, tile_of(e)].astype(BF16), qs[e]) for e in heads]
    for e in heads:
        sel_ref[e] = _select_blocks(gates[e], qi, nb, blk)

    kd = keys(qi)
    vtd = vt_ref[0, qi]
    key_i = lax.broadcasted_iota(jnp.int32, (blk, blk), 0)
    qry_i = lax.broadcasted_iota(jnp.int32, (blk, blk), 1)
    ones_rows = jnp.ones((ONES_ROWS, blk), BF16)

    def values_and_sum(vt, e):
        return jnp.concatenate([vt[e * dh:(e + 1) * dh, :], ones_rows], axis=0)

    sd = [jnp.where(key_i <= qry_i, _bdot_nt(kd[e], qs[e]), NEG_INF) for e in heads]
    md = [jnp.max(s, axis=0, keepdims=True) for s in sd]
    pd = [jnp.exp2((s - m).astype(BF16)) for s, m in zip(sd, md)]
    carry = []
    for e in heads:
        carry += [md[e], jnp.dot(values_and_sum(vtd, e), pd[e], preferred_element_type=F32)]


    @pl.when(qi == 0)
    def _():
        p_ref[...] = jnp.zeros_like(p_ref)

    def pending_values(g_prev):
        j0 = jnp.maximum(g_prev, 0) * group
        live = jnp.where(g_prev >= 0, 1.0, 0.0)
        vts = [vt_ref[0, j0 + u] for u in range(group)]
        pv = [[jnp.dot(values_and_sum(vts[u], e), p_ref[e, u], preferred_element_type=F32)
               for u in range(group)] for e in heads]
        tails = []
        for e in heads:
            tail = jnp.zeros((dh + ONES_ROWS, blk), F32)
            for u in range(group):
                on = sel_ref[e, pl.ds(j0 + u, 1), :] * live > 0.0
                tail = tail + jnp.where(on, pv[e][u], 0.0)
            tails.append(tail)
        return tails

    def attend(g, c):
        j0 = g * group
        ks = [keys(j0 + u) for u in range(group)]
        ss = [[_bdot_nt(ks[u][e], qs[e]) for u in range(group)] for e in heads]
        tails = pending_values(g - 1)
        out = []
        for e in heads:
            m, acc = c[2 * e], c[2 * e + 1]
            m_new = m
            for u in range(group):
                col_max = jnp.max(ss[e][u], axis=0, keepdims=True)
                m_new = jnp.maximum(m_new, jnp.where(sel_ref[e, pl.ds(j0 + u, 1), :] > 0.0, col_max, NEG_INF))
            for u in range(group):
                p_ref[e, u] = jnp.exp2((ss[e][u] - m_new).astype(BF16))
            out += [m_new, jnp.exp2(m - m_new) * (acc + tails[e])]
        return tuple(out)

    n_groups = (qi + group - 1) // group
    c = lax.fori_loop(0, n_groups, attend, tuple(carry))
    tails = pending_values(n_groups - 1)
    outs = []
    for e in heads:
        acc = c[2 * e + 1] + tails[e]
        outs.append(acc[0:dh, :] / acc[dh:dh + 1, :])
    tiles = [jnp.concatenate(outs[i:i + per_tile], axis=0).T for i in range(0, len(outs), per_tile)]
    o_ref[0] = jnp.concatenate(tiles, axis=1).astype(o_ref.dtype)


def _moba_prompt(q_bf, k_bf, vt, kmean, *, head_dim):
    b, t, d = q_bf.shape
    blk = MOBA_BLOCK
    assert t % blk == 0
    nb = t // blk
    nhs = MOBA_HEADS_PER_STEP
    width = nhs * head_dim
    assert d % width == 0 and width % 128 == 0
    group = KV_GROUP if nb % KV_GROUP == 0 else 1
    return pl.pallas_call(
        functools.partial(_moba_prompt_kernel, blk=blk, dh=head_dim, group=group),
        grid=(b, d // width, nb),
        in_specs=[pl.BlockSpec((1, blk, width), lambda bi, hp, qi: (bi, qi, hp)),
                  pl.BlockSpec((1, t, width), lambda bi, hp, qi: (bi, 0, hp)),
                  pl.BlockSpec((1, nb, width, blk), lambda bi, hp, qi: (bi, 0, hp, 0)),
                  pl.BlockSpec((1, nb, width), lambda bi, hp, qi: (bi, 0, hp))],
        out_specs=pl.BlockSpec((1, blk, width), lambda bi, hp, qi: (bi, qi, hp)),
        out_shape=jax.ShapeDtypeStruct((b, t, d), BF16),
        scratch_shapes=[pltpu.VMEM((nhs, nb, blk), F32), pltpu.VMEM((nhs, group, blk, blk), BF16)],
        compiler_params=_params("parallel", "parallel", "arbitrary"), name="moba_prompt",
    )(q_bf, k_bf, vt, kmean)


PAGES_PER_STEP = 8


def _sample_scores_kernel(pt_ref, q_ref, *refs):
    k_refs, o_ref = refs[:-1], refs[-1]
    q = q_ref[0]
    for u, k_ref in enumerate(k_refs):
        o_ref[0, u] = jnp.sum(k_ref[0, 0] * q, axis=1)


def _sample_scores(q_col, cache_t, layer, page_table):
    _, _, n_heads, dh, page = cache_t.shape
    b, n_pages = page_table.shape
    n_in = PAGES_PER_STEP if n_pages % PAGES_PER_STEP == 0 else 1

    def page_spec(u):
        return pl.BlockSpec((1, 1, n_heads, dh, page),
                            lambda bi, g, pt: (layer, pt[bi * n_pages + g * n_in + u], 0, 0, 0))

    return pl.pallas_call(
        _sample_scores_kernel,
        grid_spec=pltpu.PrefetchScalarGridSpec(
            num_scalar_prefetch=1, grid=(b, n_pages // n_in),
            in_specs=[pl.BlockSpec((1, n_heads, dh, 1), lambda bi, g, pt: (bi, 0, 0, 0))]
            + [page_spec(u) for u in range(n_in)],
            out_specs=pl.BlockSpec((1, n_in, n_heads, page), lambda bi, g, pt: (bi, g, 0, 0))),
        out_shape=jax.ShapeDtypeStruct((b, n_pages, n_heads, page), F32),
        compiler_params=_params("parallel", "arbitrary"), name="moba_sample_scores",
    )(page_table.reshape(-1), q_col, *([cache_t] * n_in))


def _sample_select_kernel(s_ref, idx_ref, *, per_blk):
    s = s_ref[0]
    n_pages, n_heads, _ = s.shape
    nb = n_pages // per_blk
    page_sum = jnp.sum(s, axis=-1, keepdims=True)
    gate = jnp.sum(page_sum.reshape(nb, per_blk, n_heads, 1), axis=1)
    blk_id = lax.broadcasted_iota(jnp.int32, gate.shape, 0)
    for r in range(idx_ref.shape[1]):
        top = jnp.max(gate, axis=0, keepdims=True)
        idx = jnp.min(jnp.where(gate == top, blk_id, nb), axis=0, keepdims=True)
        idx_ref[0, r] = idx[0]
        gate = jnp.where(blk_id == idx, NEG_INF, gate)


def _sample_select(scores, per_blk):
    b, n_pages, n_heads, page = scores.shape
    topk = min(MOBA_TOPK, n_pages // per_blk)
    return pl.pallas_call(
        functools.partial(_sample_select_kernel, per_blk=per_blk),
        grid=(b,),
        in_specs=[pl.BlockSpec((1, n_pages, n_heads, page), lambda bi: (bi, 0, 0, 0))],
        out_specs=pl.BlockSpec((1, topk, n_heads, 1), lambda bi: (bi, 0, 0, 0)),
        out_shape=jax.ShapeDtypeStruct((b, topk, n_heads, 1), jnp.int32),
        compiler_params=_params("parallel"), name="moba_sample_select",
    )(scores)


def _sample_attend_kernel(pg_ref, lp_ref, q_ref, kn_ref, vn_ref, s_ref, cv_hbm, o_ref, vbuf, sems,
                          *, layer, n_heads, n_sel, dh):
    bi = pl.program_id(0)
    slot = bi % 2

    def value_copies(seq, buf_slot):
        return [pltpu.make_async_copy(cv_hbm.at[layer, pg_ref[(seq * n_heads + h) * n_sel + r], h],
                                      vbuf.at[buf_slot, h, r], sems.at[buf_slot])
                for h in range(n_heads) for r in range(n_sel)]

    @pl.when(bi == 0)
    def _():
        for c in value_copies(bi, slot):
            c.start()

    @pl.when(bi + 1 < pl.num_programs(0))
    def _():
        for c in value_copies(bi + 1, 1 - slot):
            c.start()

    q = q_ref[0]
    rows, selfs = [], []
    for h in range(n_heads):
        sl = slice(h * dh, (h + 1) * dh)
        rows.append(jnp.concatenate(
            [s_ref[0, lp_ref[(bi * n_heads + h) * n_sel + r], pl.ds(h, 1), :] for r in range(n_sel)],
            axis=1))
        selfs.append(jnp.sum(q[:, sl] * kn_ref[0][:, sl], axis=-1, keepdims=True))
    s = jnp.concatenate(rows, axis=0)
    s_self = jnp.concatenate(selfs, axis=0)
    m = jnp.maximum(jnp.max(s, axis=-1, keepdims=True), s_self)
    p = jnp.exp(s - m)
    p_self = jnp.exp(s_self - m)
    inv_l = 1.0 / (jnp.sum(p, axis=-1, keepdims=True) + p_self)

    for c in value_copies(bi, slot):
        c.wait()
    pv = [_bdot_nt(jnp.broadcast_to(p[h:h + 1, :], (8, p.shape[1])),
                   jnp.concatenate([vbuf[slot, h, r] for r in range(n_sel)], axis=1))[0:1, :]
          for h in range(n_heads)]
    for h in range(n_heads):
        sl = slice(h * dh, (h + 1) * dh)
        o_ref[0, :, sl] = (pv[h] + p_self[h:h + 1, :] * vn_ref[0][:, sl]) * inv_l[h:h + 1, :]


def _sample_attend(q, k_new, v_new, scores, cache_vt, layer, phys, logical):
    b, _, d = q.shape
    _, _, n_heads, dh, page = cache_vt.shape
    n_pages = scores.shape[1]
    n_sel = phys.shape[-1]
    row_spec = pl.BlockSpec((1, 1, d), lambda bi, pg, lp: (bi, 0, 0))
    return pl.pallas_call(
        functools.partial(_sample_attend_kernel, layer=layer, n_heads=n_heads, n_sel=n_sel, dh=dh),
        grid_spec=pltpu.PrefetchScalarGridSpec(
            num_scalar_prefetch=2, grid=(b,),
            in_specs=[row_spec, row_spec, row_spec,
                      pl.BlockSpec((1, n_pages, n_heads, page), lambda bi, pg, lp: (bi, 0, 0, 0)),
                      pl.BlockSpec(memory_space=pl.ANY)],
            out_specs=row_spec,
            scratch_shapes=[pltpu.VMEM((2, n_heads, n_sel, dh, page), F32),
                            pltpu.SemaphoreType.DMA((2,))]),
        out_shape=jax.ShapeDtypeStruct((b, 1, d), F32),
        compiler_params=_params("arbitrary"), name="moba_sample_attend",
    )(phys.reshape(-1), logical.reshape(-1), q, k_new, v_new, scores, cache_vt)


def _matmul_res_kernel(a_ref, w_ref, r_ref, o_ref):
    o_ref[...] = r_ref[...] + _bdot(a_ref[...], w_ref[...])


def _matmul_res(a, w_bf, res, *, tm):
    m, k = a.shape
    n = w_bf.shape[1]
    return pl.pallas_call(
        _matmul_res_kernel, grid=(m // tm,),
        in_specs=[pl.BlockSpec((tm, k), lambda i: (i, 0)), pl.BlockSpec((k, n), lambda i: (0, 0)),
                  pl.BlockSpec((tm, n), lambda i: (i, 0))],
        out_specs=pl.BlockSpec((tm, n), lambda i: (i, 0)),
        out_shape=jax.ShapeDtypeStruct((m, n), F32),
        compiler_params=_params("parallel"), name="matmul_residual",
    )(a, w_bf, res)


def _ffn_kernel(x_ref, nw_ref, wg_ref, wu_ref, wo_ref, *refs, final_norm):
    if final_norm:
        fw_ref, o_ref, h_scr, acc_scr = refs
    else:
        o_ref, h_scr, acc_scr = refs
    j = pl.program_id(1)

    @pl.when(j == 0)
    def _():
        h_scr[...] = _rms(x_ref[...], nw_ref[...]).astype(BF16)
        acc_scr[...] = x_ref[...]

    h = h_scr[...]
    gate = jnp.dot(h, wg_ref[...], preferred_element_type=F32)
    up = jnp.dot(h, wu_ref[...], preferred_element_type=F32)
    acc_scr[...] += _bdot(_silu(gate) * up, wo_ref[...])

    @pl.when(j == pl.num_programs(1) - 1)
    def _():
        y = acc_scr[...]
        o_ref[...] = _rms(y, fw_ref[...]) if final_norm else y


def _ffn(x, nw, w_in_bf, w_out_bf, final_w=None, *, tm, n_hidden_tiles=2):
    m, d = x.shape
    hidden = w_out_bf.shape[0]
    th = hidden // n_hidden_tiles
    assert th * n_hidden_tiles == hidden and th % 128 == 0
    in_specs = [pl.BlockSpec((tm, d), lambda i, j: (i, 0)), pl.BlockSpec((1, d), lambda i, j: (0, 0)),
                pl.BlockSpec((d, th), lambda i, j: (0, j)),
                pl.BlockSpec((d, th), lambda i, j: (0, n_hidden_tiles + j)),
                pl.BlockSpec((th, d), lambda i, j: (j, 0))]
    args = [x, nw.reshape(1, d), w_in_bf, w_in_bf, w_out_bf]
    if final_w is not None:
        in_specs.append(pl.BlockSpec((1, d), lambda i, j: (0, 0)))
        args.append(final_w.reshape(1, d))
    return pl.pallas_call(
        functools.partial(_ffn_kernel, final_norm=final_w is not None),
        grid=(m // tm, n_hidden_tiles), in_specs=in_specs,
        out_specs=pl.BlockSpec((tm, d), lambda i, j: (i, 0)),
        out_shape=jax.ShapeDtypeStruct((m, d), F32),
        scratch_shapes=[pltpu.VMEM((tm, d), BF16), pltpu.VMEM((tm, d), F32)],
        compiler_params=_params("parallel", "arbitrary"), name="swiglu_ffn",
    )(*args)


def _norm_matmul_kernel(x_ref, nw_ref, w_ref, o_ref, h_scr):
    @pl.when(pl.program_id(1) == 0)
    def _():
        h_scr[...] = _rms(x_ref[...], nw_ref[...]).astype(BF16)

    o_ref[...] = jnp.dot(h_scr[...], w_ref[...], preferred_element_type=F32)


def _norm_matmul(x, nw, w_bf, *, tm, tn):
    m, d = x.shape
    n = w_bf.shape[1]
    return pl.pallas_call(
        _norm_matmul_kernel, grid=(m // tm, n // tn),
        in_specs=[pl.BlockSpec((tm, d), lambda i, j: (i, 0)), pl.BlockSpec((1, d), lambda i, j: (0, 0)),
                  pl.BlockSpec((d, tn), lambda i, j: (0, j))],
        out_specs=pl.BlockSpec((tm, tn), lambda i, j: (i, j)),
        out_shape=jax.ShapeDtypeStruct((m, n), F32),
        scratch_shapes=[pltpu.VMEM((tm, d), BF16)],
        compiler_params=_params("parallel", "arbitrary"), name="norm_matmul",
    )(x, nw.reshape(1, d), w_bf)


def _gates_kernel(x_ref, nw_ref, w_ref, alog_ref, dtb_ref, beta_ref, g_ref, gc_ref, *, nh, chunk):
    h = _rms(x_ref[...], nw_ref[...])
    ba = jnp.dot(h, w_ref[...], precision=HI, preferred_element_type=F32)
    b_in, a_in = ba[:, 0:nh], ba[:, LANES:LANES + nh]
    z = a_in + dtb_ref[...]
    softplus = jnp.maximum(z, 0.0) + jnp.log1p(jnp.exp(-jnp.abs(z)))
    g = -jnp.exp(alog_ref[...]) * softplus
    beta_ref[...] = jax.nn.sigmoid(b_in)
    g_ref[...] = g
    tm = g.shape[0]
    if chunk == 1:
        gc_ref[...] = g
    else:
        r = lax.broadcasted_iota(jnp.int32, (chunk, chunk), 0)
        c = lax.broadcasted_iota(jnp.int32, (chunk, chunk), 1)
        tri = jnp.where(r >= c, 1.0, 0.0)
        for i in range(tm // chunk):
            gc_ref[i * chunk:(i + 1) * chunk, :] = jnp.dot(
                tri, g[i * chunk:(i + 1) * chunk, :], precision=HI, preferred_element_type=F32)


def _gates(x, nw, w_ba, a_log, dt_bias, *, tm, chunk):
    m, d = x.shape
    nh = a_log.shape[0]
    spec = pl.BlockSpec((tm, nh), lambda i: (i, 0))
    small = pl.BlockSpec((1, nh), lambda i: (0, 0))
    return pl.pallas_call(
        functools.partial(_gates_kernel, nh=nh, chunk=chunk), grid=(m // tm,),
        in_specs=[pl.BlockSpec((tm, d), lambda i: (i, 0)), pl.BlockSpec((1, d), lambda i: (0, 0)),
                  pl.BlockSpec(w_ba.shape, lambda i: (0, 0)), small, small],
        out_specs=[spec, spec, spec],
        out_shape=[jax.ShapeDtypeStruct((m, nh), F32)] * 3,
        compiler_params=_params("parallel"), name="delta_gates",
    )(x, nw.reshape(1, d), w_ba, a_log.reshape(1, nh), dt_bias.reshape(1, nh))


def _post_conv(conv, q_ref, k_ref, v_ref, *, key_dim, dk):
    y = _silu(conv)
    for j in range(key_dim // dk):
        for ref, off, scale in ((q_ref, 0, dk ** -0.5), (k_ref, key_dim, 1.0)):
            t = y[:, off + j * dk:off + (j + 1) * dk]
            inv = lax.rsqrt(jnp.sum(t * t, axis=-1, keepdims=True) + NORM_EPS)
            ref[0, :, j * dk:(j + 1) * dk] = t * (inv * scale)
    v_ref[0] = y[:, 2 * key_dim:]


SUBLANES = 8


def _delta_inproj_kernel(x_ref, nw_ref, w_ref, cw_ref, q_ref, k_ref, v_ref, z_ref, tail_ref, halo,
                         *, key_dim, dk, tiles_per_seq):
    i = pl.program_id(0)
    width, conv_dim = cw_ref.shape
    tm = x_ref.shape[0]

    @pl.when(i % tiles_per_seq == 0)
    def _():
        halo[...] = jnp.zeros_like(halo)

    h = _rms(x_ref[...], nw_ref[...]).astype(BF16)
    row = lax.broadcasted_iota(jnp.int32, (SUBLANES, key_dim), 0)
    for j in range(w_ref.shape[1] // key_dim):
        cols = slice(j * key_dim, (j + 1) * key_dim)
        acc = jnp.dot(h, w_ref[:, cols], preferred_element_type=F32)
        if j * key_dim >= conv_dim:
            z_ref[:, j * key_dim - conv_dim:(j + 1) * key_dim - conv_dim] = acc
            continue
        prev = halo[:, cols]
        conv = acc * cw_ref[width - 1:width, cols]
        for s in range(1, width):
            rolled = pltpu.roll(acc, s, 0)
            head = jnp.where(row < s, pltpu.roll(prev, s, 0), rolled[0:SUBLANES])
            conv = conv + jnp.concatenate([head, rolled[SUBLANES:]], axis=0) * cw_ref[width - 1 - s:width - s, cols]
        halo[:, cols] = acc[tm - SUBLANES:tm]
        tail_ref[0, :, cols] = acc[tm - SUBLANES:tm]
        y = _silu(conv)
        if j >= 2:
            v_ref[:, (j - 2) * key_dim:(j - 1) * key_dim] = y
            continue
        ref, scale = (q_ref, dk ** -0.5) if j == 0 else (k_ref, 1.0)
        for hd in range(key_dim // dk):
            t = y[:, hd * dk:(hd + 1) * dk]
            inv = lax.rsqrt(jnp.sum(t * t, axis=-1, keepdims=True) + NORM_EPS)
            ref[:, hd * dk:(hd + 1) * dk] = t * (inv * scale)


def _delta_inproj(x, nw, w_bf, conv_w, *, seq_len, key_dim, dk, tm):
    m, d = x.shape
    width, conv_dim = conv_w.shape
    n = w_bf.shape[1]
    val_dim = conv_dim - 2 * key_dim
    assert n == conv_dim + val_dim and val_dim % key_dim == 0 and seq_len % tm == 0 and tm > SUBLANES
    tiles_per_seq = seq_len // tm
    rows = lambda c: pl.BlockSpec((tm, c), lambda i: (i, 0))
    return pl.pallas_call(
        functools.partial(_delta_inproj_kernel, key_dim=key_dim, dk=dk, tiles_per_seq=tiles_per_seq),
        grid=(m // tm,),
        in_specs=[rows(d), pl.BlockSpec((1, d), lambda i: (0, 0)),
                  pl.BlockSpec((d, n), lambda i: (0, 0), pipeline_mode=pl.Buffered(1)),
                  pl.BlockSpec((width, conv_dim), lambda i: (0, 0))],
        out_specs=[rows(key_dim), rows(key_dim), rows(val_dim), rows(val_dim),
                   pl.BlockSpec((1, SUBLANES, conv_dim), lambda i: (i // tiles_per_seq, 0, 0))],
        out_shape=[jax.ShapeDtypeStruct((m, key_dim), F32), jax.ShapeDtypeStruct((m, key_dim), F32),
                   jax.ShapeDtypeStruct((m, val_dim), F32), jax.ShapeDtypeStruct((m, val_dim), F32),
                   jax.ShapeDtypeStruct((m // seq_len, SUBLANES, conv_dim), F32)],
        scratch_shapes=[pltpu.VMEM((SUBLANES, conv_dim), F32)],
        compiler_params=_params("arbitrary"), name="delta_inproj_conv",
    )(x, nw.reshape(1, d), w_bf, conv_w)


def _conv_step_kernel(hist_ref, cur_ref, w_ref, q_ref, k_ref, v_ref, *, key_dim, dk, width):
    conv = cur_ref[...] * w_ref[width - 1:width, :]
    for i in range(width - 1):
        conv = conv + hist_ref[i] * w_ref[i:i + 1, :]
    _post_conv(conv, q_ref, k_ref, v_ref, key_dim=key_dim, dk=dk)


def _conv_step(hist_t, cur, conv_w, *, key_dim, dk):
    _, b, c = hist_t.shape
    width = conv_w.shape[0]
    val_dim = c - 2 * key_dim
    return pl.pallas_call(
        functools.partial(_conv_step_kernel, key_dim=key_dim, dk=dk, width=width),
        out_shape=[jax.ShapeDtypeStruct((1, b, key_dim), F32), jax.ShapeDtypeStruct((1, b, key_dim), F32),
                   jax.ShapeDtypeStruct((1, b, val_dim), F32)],
        compiler_params=pltpu.CompilerParams(vmem_limit_bytes=VMEM_LIMIT_BYTES), name="delta_conv_step",
    )(hist_t, cur, conv_w)


def _unit_lower_inverse(mats, row, col):
    c = mats[0].shape[0]
    shift = INV_BASE.bit_length() - 1
    same = lax.shift_right_logical(row, shift) == lax.shift_right_logical(col, shift)
    eye = jnp.where(row == col, 1.0, 0.0)
    power = [jnp.where(same, a, 0.0) for a in mats]
    xs = [eye - p for p in power]
    for _ in range(shift - 1):
        power = [_bdot(p, p) for p in power]
        xs = [x + _bdot(x, p) for x, p in zip(xs, power)]
    size = INV_BASE
    while size < c:
        shift += 1
        wider = lax.shift_right_logical(row, shift) == lax.shift_right_logical(col, shift)
        off = wider & jnp.logical_not(same)
        ex = [_bdot(jnp.where(off, a, 0.0), x) for a, x in zip(mats, xs)]
        xs = [x - _bdot(x, e) for x, e in zip(xs, ex)]
        same = wider
        size *= 2
    return xs


def _delta_chunk_kernel(q_ref, k_ref, v_ref, beta_ref, g_ref, gt_ref, s0_ref,
                        o_ref, s_ref, state, *, chunk, dk, dv):
    ci = pl.program_id(1)

    @pl.when(ci == 0)
    def _():
        state[...] = s0_ref[...]

    row = lax.broadcasted_iota(jnp.int32, (chunk, chunk), 0)
    col = lax.broadcasted_iota(jnp.int32, (chunk, chunk), 1)
    causal = row >= col
    n_seq, nh = state.shape[:2]
    hk = q_ref.shape[2] // dk
    rep = nh // hk
    chains = [(s, h) for s in range(n_seq) for h in range(nh)]
    groups = [(s, j) for s in range(n_seq) for j in range(hk)]
    q = {(s, j): q_ref[s, :, j * dk:(j + 1) * dk] for s, j in groups}
    k = {(s, j): k_ref[s, :, j * dk:(j + 1) * dk] for s, j in groups}
    kt = {c: k[c].T for c in groups}
    kk = {c: _bdot(k[c], kt[c]) for c in groups}
    qk = {c: _bdot(q[c], kt[c]) for c in groups}
    beta_all = [beta_ref[s] for s in range(n_seq)]
    g_all = [g_ref[s] for s in range(n_seq)]
    gt_all = [gt_ref[s, 0] for s in range(n_seq)]
    beta = {(s, h): beta_all[s][:, h:h + 1] for s, h in chains}
    gcol = {(s, h): g_all[s][:, h:h + 1] for s, h in chains}
    grow = {(s, h): gt_all[s][h:h + 1, :] for s, h in chains}
    decay = {c: jnp.where(causal, jnp.exp(jnp.where(causal, gcol[c] - grow[c], 0.0)), 0.0)
             for c in chains}
    a = [jnp.where(row > col, beta[s, h] * kk[s, h // rep] * decay[s, h], 0.0) for s, h in chains]
    t_inv = _unit_lower_inverse(a, row, col)
    t_off = {c: jnp.where(row == col, 0.0, t) for c, t in zip(chains, t_inv)}
    eg = {c: jnp.exp(gcol[c]) for c in chains}
    bv = {(s, h): beta[s, h] * v_ref[s, :, h * dv:(h + 1) * dv] for s, h in chains}
    bk = {(s, h): (beta[s, h] * eg[s, h]) * k[s, h // rep] for s, h in chains}
    u_val = {c: bv[c] + _bdot(t_off[c], bv[c]) for c in chains}
    w = {c: bk[c] + _bdot(t_off[c], bk[c]) for c in chains}
    s_old = {(s, h): state[s, h] for s, h in chains}
    u = {c: u_val[c] - _bdot(w[c], s_old[c]) for c in chains}
    o_state = {(s, h): _bdot(q[s, h // rep] * eg[s, h], s_old[s, h]) for s, h in chains}
    o_local = {(s, h): _bdot(qk[s, h // rep] * decay[s, h], u[s, h]) for s, h in chains}
    g_last = {c: gcol[c][chunk - 1:chunk, :] for c in chains}
    s_add = {(s, h): _bdot(kt[s, h // rep] * jnp.exp(g_last[s, h] - grow[s, h]), u[s, h])
             for s, h in chains}
    for s, h in chains:
        o_ref[s, :, h * dv:(h + 1) * dv] = o_state[s, h] + o_local[s, h]
        state[s, h] = s_old[s, h] * jnp.exp(g_last[s, h]) + s_add[s, h]

    @pl.when(ci == pl.num_programs(1) - 1)
    def _():
        s_ref[...] = state[...]


DELTA_SEQS_PER_STEP = 2


def _delta_chunked(qn, kn, v, beta, gc, s0, *, dk, dv, chunk):
    b, t, key_dim = qn.shape
    nh = beta.shape[-1]
    nc = t // chunk
    ns = DELTA_SEQS_PER_STEP if b % DELTA_SEQS_PER_STEP == 0 else 1
    gt = gc.reshape(b, nc, chunk, nh).transpose(0, 1, 3, 2)
    tok = lambda n: pl.BlockSpec((ns, chunk, n), lambda bi, c: (bi, c, 0))
    st = pl.BlockSpec((ns, nh, dk, dv), lambda bi, c: (bi, 0, 0, 0))
    return pl.pallas_call(
        functools.partial(_delta_chunk_kernel, chunk=chunk, dk=dk, dv=dv),
        grid=(b // ns, nc),
        in_specs=[tok(key_dim), tok(key_dim), tok(nh * dv), tok(nh), tok(nh),
                  pl.BlockSpec((ns, 1, nh, chunk), lambda bi, c: (bi, c, 0, 0)), st],
        out_specs=[tok(nh * dv), st],
        out_shape=[jax.ShapeDtypeStruct((b, t, nh * dv), F32),
                   jax.ShapeDtypeStruct((b, nh, dk, dv), F32)],
        scratch_shapes=[pltpu.VMEM((ns, nh, dk, dv), F32)],
        compiler_params=_params("parallel", "arbitrary"), name="delta_chunked",
    )(qn, kn, v, beta, gc, gt, s0)


def _delta_step_kernel(q_ref, k_ref, qc_ref, kc_ref, v_ref, beta_ref, g_ref, s0_ref, o_ref, s_ref,
                       *, dk, dv, rep):
    nh = s0_ref.shape[1]
    for hv in range(nh):
        j = hv // rep
        q_row = q_ref[0][:, j * dk:(j + 1) * dk]
        k_row = k_ref[0][:, j * dk:(j + 1) * dk]
        q_col = qc_ref[0, j * dk:(j + 1) * dk, :]
        k_col = kc_ref[0, j * dk:(j + 1) * dk, :]
        beta = beta_ref[0][:, hv:hv + 1]
        eg = jnp.exp(g_ref[0][:, hv:hv + 1])
        s_old = s0_ref[0, hv]
        k_s = jnp.sum(s_old * k_col, axis=0, keepdims=True)
        q_s = jnp.sum(s_old * q_col, axis=0, keepdims=True)
        u = beta * (v_ref[0][:, hv * dv:(hv + 1) * dv] - eg * k_s)
        qk = jnp.sum(q_row * k_row, axis=-1, keepdims=True)
        o_ref[0, :, hv * dv:(hv + 1) * dv] = eg * q_s + qk * u
        s_ref[0, hv] = s_old * eg + k_col * u


def _delta_step(qn, kn, v, beta, g, s0, *, dk, dv):
    b, _, key_dim = qn.shape
    nh = beta.shape[-1]
    rep = nh // (key_dim // dk)
    row = lambda n: pl.BlockSpec((1, 1, n), lambda bi: (bi, 0, 0))
    col = pl.BlockSpec((1, key_dim, 1), lambda bi: (bi, 0, 0))
    st = pl.BlockSpec((1, nh, dk, dv), lambda bi: (bi, 0, 0, 0))
    return pl.pallas_call(
        functools.partial(_delta_step_kernel, dk=dk, dv=dv, rep=rep), grid=(b,),
        in_specs=[row(key_dim), row(key_dim), col, col, row(nh * dv), row(nh), row(nh), st],
        out_specs=[row(nh * dv), st],
        out_shape=[jax.ShapeDtypeStruct((b, 1, nh * dv), F32), jax.ShapeDtypeStruct(s0.shape, F32)],
        compiler_params=_params("parallel"), name="delta_step",
    )(qn, kn, qn.reshape(b, key_dim, 1), kn.reshape(b, key_dim, 1), v, beta, g, s0)


def _gated_out_kernel(o_ref, z_ref, gw_ref, w_ref, r_ref, y_ref, a_scr, *, dv):
    gw = gw_ref[...]
    for hv in range(o_ref.shape[1] // dv):
        sl = slice(hv * dv, (hv + 1) * dv)
        o = o_ref[:, sl]
        inv = lax.rsqrt(jnp.mean(o * o, axis=-1, keepdims=True) + NORM_EPS)
        a_scr[:, sl] = (o * inv * gw * _silu(z_ref[:, sl])).astype(BF16)
    y_ref[...] = r_ref[...] + jnp.dot(a_scr[...], w_ref[...], preferred_element_type=F32)


def _gated_out(o, proj, z_col_block, gnorm_w, w_out_bf, res, *, tm, dv):
    m, val_dim = o.shape
    n = w_out_bf.shape[1]
    return pl.pallas_call(
        functools.partial(_gated_out_kernel, dv=dv), grid=(m // tm,),
        in_specs=[pl.BlockSpec((tm, val_dim), lambda i: (i, 0)),
                  pl.BlockSpec((tm, val_dim), lambda i: (i, z_col_block)),
                  pl.BlockSpec((1, dv), lambda i: (0, 0)),
                  pl.BlockSpec((val_dim, n), lambda i: (0, 0)),
                  pl.BlockSpec((tm, n), lambda i: (i, 0))],
        out_specs=pl.BlockSpec((tm, n), lambda i: (i, 0)),
        out_shape=jax.ShapeDtypeStruct((m, n), F32),
        scratch_shapes=[pltpu.VMEM((tm, val_dim), BF16)],
        compiler_params=_params("parallel"), name="delta_gated_out",
    )(o, proj, gnorm_w.reshape(1, dv), w_out_bf, res)


def kernel(x_prompt, x_sample, cache_k, cache_v, page_table, state_conv, state_ssm, norm_mix, w_qkv_attn, w_o_attn, w_in_delta, conv_w_delta, a_log_delta, dt_bias_delta, gnorm_delta, w_out_delta, norm_ffn, w_ffn_in, w_ffn_out, norm_final):
    b, t, d = x_prompt.shape
    sb = x_sample.shape[0]
    assert x_sample.shape[1] == 1, "the sample group advances one token per sequence"
    n_heads, head_dim = cache_k.shape[-2:]
    page = cache_k.shape[2]
    n_pages = page_table.shape[1]
    past_len = n_pages * page
    assert past_len % MOBA_BLOCK == 0 and MOBA_BLOCK % page == 0 and t % MOBA_BLOCK == 0
    depth = norm_mix.shape[0]
    assert depth == 2, "layer 0 is attention, layer 1 is the gated DeltaNet"
    nh_v, dk, dv = state_ssm.shape[2:]
    conv_dim = conv_w_delta.shape[-1]
    val_dim = nh_v * dv
    key_dim = (conv_dim - val_dim) // 2
    assert conv_dim % val_dim == 0 and t % DN_CHUNK == 0

    xp = x_prompt.reshape(b * t, d)
    xs = x_sample.reshape(sb, d)
    tm_p = _row_tile(b * t, 512)

    w_qkv = w_qkv_attn[0].astype(BF16)
    w_o = w_o_attn[0].astype(BF16)
    nb_p = t // MOBA_BLOCK
    q_p, k_p, v_p, kb_p, vt_p, km_p = _qkv_rope(
        xp, norm_mix[0], w_qkv, jnp.arange(t, dtype=jnp.int32), tm=MOBA_BLOCK, head_dim=head_dim,
        q_scale=head_dim ** -0.5 * math.log2(math.e), q_dtype=BF16, block_outputs=True)
    o_p = _moba_prompt(q_p.reshape(b, t, d), kb_p.reshape(b, t, d), vt_p.reshape(b, nb_p, d, MOBA_BLOCK),
                       km_p.reshape(b, nb_p, d), head_dim=head_dim)
    xp = _matmul_res(o_p.reshape(b * t, d), w_o, xp, tm=tm_p)

    q_s, k_s, v_s = _qkv_rope(
        xs, norm_mix[0], w_qkv, jnp.full((sb,), past_len, jnp.int32), tm=sb, head_dim=head_dim,
        q_scale=head_dim ** -0.5, q_dtype=F32, block_outputs=False)
    per_blk = MOBA_BLOCK // page
    cache_kt = cache_k.transpose(0, 1, 3, 4, 2)
    cache_vt = cache_v.transpose(0, 1, 3, 4, 2)
    scores = _sample_scores(q_s.reshape(sb, n_heads, head_dim, 1), cache_kt, 0, page_table)
    sel = _sample_select(scores, per_blk)[..., 0]
    logical = (sel.transpose(0, 2, 1)[..., None] * per_blk
               + jnp.arange(per_blk, dtype=jnp.int32)).reshape(sb, n_heads, -1)
    phys = jnp.take_along_axis(page_table, logical.reshape(sb, -1), axis=1).reshape(sb, n_heads, -1)
    o_s = _sample_attend(q_s.reshape(sb, 1, d), k_s.reshape(sb, 1, d), v_s.reshape(sb, 1, d),
                         scores, cache_vt, 0, phys, logical)
    xs = _matmul_res(o_s.reshape(sb, d), w_o, xs, tm=sb)

    xp = _ffn(xp, norm_ffn[0], w_ffn_in[0].astype(BF16), w_ffn_out[0].astype(BF16), tm=tm_p)
    xs = _ffn(xs, norm_ffn[0], w_ffn_in[0].astype(BF16), w_ffn_out[0].astype(BF16), tm=sb)

    w_in = w_in_delta[0]
    main = conv_dim + val_dim
    w_main = w_in[:, :main].astype(BF16)
    w_ba = jnp.zeros((d, 2 * LANES), F32).at[:, :nh_v].set(w_in[:, main:main + nh_v])
    w_ba = w_ba.at[:, LANES:LANES + nh_v].set(w_in[:, main + nh_v:])
    w_out = w_out_delta[0].astype(BF16)
    conv_w = conv_w_delta[0]
    z_block = conv_dim // val_dim

    qn_p, kn_p, vv_p, z_p, tail_p = _delta_inproj(xp, norm_mix[1], w_main, conv_w, seq_len=t,
                                                  key_dim=key_dim, dk=dk, tm=256)
    beta_p, _, gc_p = _gates(xp, norm_mix[1], w_ba, a_log_delta[0], dt_bias_delta[0],
                             tm=tm_p, chunk=DN_CHUNK)
    o_dp, ssm_p = _delta_chunked(qn_p.reshape(b, t, key_dim), kn_p.reshape(b, t, key_dim),
                                 vv_p.reshape(b, t, val_dim), beta_p.reshape(b, t, nh_v),
                                 gc_p.reshape(b, t, nh_v), jnp.zeros((b,) + state_ssm.shape[2:], F32),
                                 dk=dk, dv=dv, chunk=DN_CHUNK)
    xp = _gated_out(o_dp.reshape(b * t, val_dim), z_p, 0, gnorm_delta[0], w_out, xp, tm=tm_p, dv=dv)
    conv_p = tail_p[:, SUBLANES - (conv_w.shape[0] - 1):, :]

    proj_s = _norm_matmul(xs, norm_mix[1], w_main, tm=sb, tn=1024)
    beta_s, g_s, _ = _gates(xs, norm_mix[1], w_ba, a_log_delta[0], dt_bias_delta[0], tm=sb, chunk=1)
    hist_t = state_conv[0].transpose(1, 0, 2)
    cur_s = proj_s[:, :conv_dim]
    qn_s, kn_s, vv_s = _conv_step(hist_t, cur_s, conv_w, key_dim=key_dim, dk=dk)
    o_ds, ssm_s = _delta_step(qn_s.reshape(sb, 1, key_dim), kn_s.reshape(sb, 1, key_dim),
                              vv_s.reshape(sb, 1, val_dim), beta_s.reshape(sb, 1, nh_v),
                              g_s.reshape(sb, 1, nh_v), state_ssm[0], dk=dk, dv=dv)
    xs = _gated_out(o_ds.reshape(sb, val_dim), proj_s, z_block, gnorm_delta[0], w_out, xs, tm=sb, dv=dv)
    conv_s = jnp.concatenate([state_conv[0][:, 1:], cur_s[:, None, :]], axis=1)

    y_p = _ffn(xp, norm_ffn[1], w_ffn_in[1].astype(BF16), w_ffn_out[1].astype(BF16), norm_final, tm=tm_p)
    y_s = _ffn(xs, norm_ffn[1], w_ffn_in[1].astype(BF16), w_ffn_out[1].astype(BF16), norm_final, tm=sb)

    kv_p = (b, t, n_heads, head_dim)
    kv_s = (sb, 1, n_heads, head_dim)
    return (y_p.reshape(b, t, d), y_s.reshape(sb, 1, d),
            k_p.reshape(kv_p)[None], v_p.reshape(kv_p)[None],
            k_s.reshape(kv_s)[None], v_s.reshape(kv_s)[None],
            conv_p[None], ssm_p[None], conv_s[None], ssm_s[None])
```

```python
import functools
import math

import jax
import jax.numpy as jnp
from jax import lax
from jax.experimental import pallas as pl
from jax.experimental.pallas import tpu as pltpu

F32 = jnp.float32
BF16 = jnp.bfloat16

NORM_EPS = 1e-6
ROPE_THETA = 10000.0
MOBA_BLOCK = 256
MOBA_TOPK = 3
DN_CHUNK = 64
KV_GROUP = 2
ONES_ROWS = 16
MOBA_HEADS_PER_STEP = 4
LANES = 128
INV_BASE = 8
VMEM_LIMIT_BYTES = 48 * 1024 * 1024
NEG_INF = float("-inf")
HI = lax.Precision.HIGHEST


def _params(*sem):
    return pltpu.CompilerParams(dimension_semantics=sem, vmem_limit_bytes=VMEM_LIMIT_BYTES)


def _rms(x, w):
    ms = jnp.mean(x * x, axis=-1, keepdims=True)
    return x * lax.rsqrt(ms + NORM_EPS) * w


def _bdot(a, b):
    return jnp.dot(a.astype(BF16), b.astype(BF16), preferred_element_type=F32)


def _bdot_nt(a, b):
    return lax.dot_general(a.astype(BF16), b.astype(BF16), (((1,), (1,)), ((), ())),
                           preferred_element_type=F32)


def _silu(x):
    return x * jax.nn.sigmoid(x)


def _row_tile(m, want):
    return want if m % want == 0 else m


def _qkv_rope_kernel(x_ref, nw_ref, w_ref, cos_ref, sa_ref, sb_ref,
                     q_ref, k_ref, v_ref, *block_refs, d, half, q_scale):
    h = _rms(x_ref[...], nw_ref[...]).astype(BF16)
    rep = d // cos_ref.shape[-1]
    cos = jnp.concatenate([cos_ref[...]] * rep, axis=1)
    sa = jnp.concatenate([sa_ref[...]] * rep, axis=1)
    sb = jnp.concatenate([sb_ref[...]] * rep, axis=1)

    def rope(t):
        return t * cos + pltpu.roll(t, d - half, 1) * sa + pltpu.roll(t, half, 1) * sb

    q = rope(jnp.dot(h, w_ref[:, 0:d], preferred_element_type=F32))
    k = rope(jnp.dot(h, w_ref[:, d:2 * d], preferred_element_type=F32))
    v = jnp.dot(h, w_ref[:, 2 * d:3 * d], preferred_element_type=F32)
    q_ref[...] = (q * q_scale).astype(q_ref.dtype)
    k_ref[...] = k
    v_ref[...] = v
    if block_refs:
        kb_ref, vt_ref, km_ref = block_refs
        kb_ref[...] = k.astype(BF16)
        vt_ref[0] = v.T.astype(BF16)
        km_ref[0] = jnp.mean(k, axis=0, keepdims=True)


def _rope_tables(pos, head_dim, lanes):
    half = head_dim // 2
    inv_freq = ROPE_THETA ** (-jnp.arange(half, dtype=F32) / half)
    ang = pos.astype(F32)[:, None] * inv_freq[None, :]
    reps = lanes // half
    cos = jnp.tile(jnp.cos(ang), (1, reps))
    sin = jnp.tile(jnp.sin(ang), (1, reps))
    first = (jnp.arange(lanes) % head_dim) < half
    return cos, jnp.where(first, -sin, 0.0), jnp.where(first, 0.0, sin)


def _qkv_rope(x, nw, w_bf, pos, *, tm, head_dim, q_scale, q_dtype, block_outputs):
    m, d = x.shape
    n_pos = pos.shape[0]
    lanes = 2 * head_dim
    cos, sa, sb = _rope_tables(pos, head_dim, lanes)
    pos_tiles = n_pos // tm
    tab_spec = pl.BlockSpec((tm, lanes), lambda i: (i % pos_tiles, 0))
    row_spec = pl.BlockSpec((tm, d), lambda i: (i, 0))
    out_shape = [jax.ShapeDtypeStruct((m, d), q_dtype), jax.ShapeDtypeStruct((m, d), F32),
                 jax.ShapeDtypeStruct((m, d), F32)]
    out_specs = [row_spec] * 3
    if block_outputs:
        assert tm == MOBA_BLOCK
        out_shape += [jax.ShapeDtypeStruct((m, d), BF16), jax.ShapeDtypeStruct((m // tm, d, tm), BF16),
                      jax.ShapeDtypeStruct((m // tm, 1, d), F32)]
        out_specs += [row_spec, pl.BlockSpec((1, d, tm), lambda i: (i, 0, 0)),
                      pl.BlockSpec((1, 1, d), lambda i: (i, 0, 0))]
    return pl.pallas_call(
        functools.partial(_qkv_rope_kernel, d=d, half=head_dim // 2, q_scale=q_scale),
        grid=(m // tm,),
        in_specs=[row_spec, pl.BlockSpec((1, d), lambda i: (0, 0)),
                  pl.BlockSpec((d, 3 * d), lambda i: (0, 0)), tab_spec, tab_spec, tab_spec],
        out_specs=out_specs, out_shape=out_shape,
        compiler_params=_params("parallel"), name="qkv_rope",
    )(x, nw.reshape(1, d), w_bf, cos, sa, sb)


def _select_blocks(gate, n_past, nb, width):
    blk_id = lax.broadcasted_iota(jnp.int32, (nb, width), 0)
    past = blk_id < n_past
    g = jnp.where(past, gate, NEG_INF)
    sel = jnp.zeros((nb, width), F32)
    for _ in range(min(MOBA_TOPK, nb)):
        top = jnp.max(g, axis=0, keepdims=True)
        idx = jnp.min(jnp.where(g == top, blk_id, nb), axis=0, keepdims=True)
        hit = blk_id == idx
        sel = jnp.where(hit, 1.0, sel)
        g = jnp.where(hit, NEG_INF, g)
    return jnp.where(past, sel, 0.0)


def _moba_prompt_kernel(q_ref, k_ref, vt_ref, km_ref, o_ref, sel_ref, p_ref, *, blk, dh, group):
    qi = pl.program_id(2)
    heads = range(q_ref.shape[2] // dh)
    per_tile = LANES // dh
    tile_of = lambda e: slice((e // per_tile) * LANES, (e // per_tile + 1) * LANES)
    nb = km_ref.shape[1]
    lane = lax.broadcasted_iota(jnp.int32, (blk, LANES), 1)
    qs = []
    for e in heads:
        qt = q_ref[0, :, tile_of(e)]
        lo = (e % per_tile) * dh
        qs.append(jnp.where((lane >= lo) & (lane < lo + dh), qt, jnp.zeros_like(qt)))

    def keys(j):
        rows = pl.ds(pl.multiple_of(j * blk, blk), blk)
        return [k_ref[0, rows, tile_of(e)] for e in heads]

    gates = [_bdot_nt(km_ref[0, :, tile_of(e)].astype(BF16), qs[e]) for e in heads]
    for e in heads:
        sel_ref[e] = _select_blocks(gates[e], qi, nb, blk)

    kd = keys(qi)
    vtd = vt_ref[0, qi]
    key_i = lax.broadcasted_iota(jnp.int32, (blk, blk), 0)
    qry_i = lax.broadcasted_iota(jnp.int32, (blk, blk), 1)
    ones_rows = jnp.ones((ONES_ROWS, blk), BF16)

    def values_and_sum(vt, e):
        return jnp.concatenate([vt[e * dh:(e + 1) * dh, :], ones_rows], axis=0)

    sd = [jnp.where(key_i <= qry_i, _bdot_nt(kd[e], qs[e]), NEG_INF) for e in heads]
    md = [jnp.max(s, axis=0, keepdims=True) for s in sd]
    pd = [jnp.exp2((s - m).astype(BF16)) for s, m in zip(sd, md)]
    carry = []
    for e in heads:
        carry += [md[e], jnp.dot(values_and_sum(vtd, e), pd[e], preferred_element_type=F32)]


    @pl.when(qi == 0)
    def _():
        p_ref[...] = jnp.zeros_like(p_ref)

    def pending_values(g_prev):
        j0 = jnp.maximum(g_prev, 0) * group
        live = jnp.where(g_prev >= 0, 1.0, 0.0)
        vts = [vt_ref[0, j0 + u] for u in range(group)]
        pv = [[jnp.dot(values_and_sum(vts[u], e), p_ref[e, u], preferred_element_type=F32)
               for u in range(group)] for e in heads]
        tails = []
        for e in heads:
            tail = jnp.zeros((dh + ONES_ROWS, blk), F32)
            for u in range(group):
                on = sel_ref[e, pl.ds(j0 + u, 1), :] * live > 0.0
                tail = tail + jnp.where(on, pv[e][u], 0.0)
            tails.append(tail)
        return tails

    def attend(g, c):
        j0 = g * group
        ks = [keys(j0 + u) for u in range(group)]
        ss = [[_bdot_nt(ks[u][e], qs[e]) for u in range(group)] for e in heads]
        tails = pending_values(g - 1)
        out = []
        for e in heads:
            m, acc = c[2 * e], c[2 * e + 1]
            m_new = m
            for u in range(group):
                col_max = jnp.max(ss[e][u], axis=0, keepdims=True)
                m_new = jnp.maximum(m_new, jnp.where(sel_ref[e, pl.ds(j0 + u, 1), :] > 0.0, col_max, NEG_INF))
            for u in range(group):
                p_ref[e, u] = jnp.exp2((ss[e][u] - m_new).astype(BF16))
            out += [m_new, jnp.exp2(m - m_new) * (acc + tails[e])]
        return tuple(out)

    n_groups = (qi + group - 1) // group
    c = lax.fori_loop(0, n_groups, attend, tuple(carry))
    tails = pending_values(n_groups - 1)
    outs = []
    for e in heads:
        acc = c[2 * e + 1] + tails[e]
        outs.append(acc[0:dh, :] / acc[dh:dh + 1, :])
    tiles = [jnp.concatenate(outs[i:i + per_tile], axis=0).T for i in range(0, len(outs), per_tile)]
    o_ref[0] = jnp.concatenate(tiles, axis=1).astype(o_ref.dtype)


def _moba_prompt(q_bf, k_bf, vt, kmean, *, head_dim):
    b, t, d = q_bf.shape
    blk = MOBA_BLOCK
    assert t % blk == 0
    nb = t // blk
    nhs = MOBA_HEADS_PER_STEP
    width = nhs * head_dim
    assert d % width == 0 and width % 128 == 0
    group = KV_GROUP if nb % KV_GROUP == 0 else 1
    return pl.pallas_call(
        functools.partial(_moba_prompt_kernel, blk=blk, dh=head_dim, group=group),
        grid=(b, d // width, nb),
        in_specs=[pl.BlockSpec((1, blk, width), lambda bi, hp, qi: (bi, qi, hp)),
                  pl.BlockSpec((1, t, width), lambda bi, hp, qi: (bi, 0, hp)),
                  pl.BlockSpec((1, nb, width, blk), lambda bi, hp, qi: (bi, 0, hp, 0)),
                  pl.BlockSpec((1, nb, width), lambda bi, hp, qi: (bi, 0, hp))],
        out_specs=pl.BlockSpec((1, blk, width), lambda bi, hp, qi: (bi, qi, hp)),
        out_shape=jax.ShapeDtypeStruct((b, t, d), BF16),
        scratch_shapes=[pltpu.VMEM((nhs, nb, blk), F32), pltpu.VMEM((nhs, group, blk, blk), BF16)],
        compiler_params=_params("parallel", "parallel", "arbitrary"), name="moba_prompt",
    )(q_bf, k_bf, vt, kmean)


PAGES_PER_STEP = 16


def _sample_scores_kernel(pt_ref, q_ref, *refs):
    k_refs, o_ref = refs[:-1], refs[-1]
    q = q_ref[0]
    for u, k_ref in enumerate(k_refs):
        o_ref[0, u] = jnp.sum(k_ref[0, 0] * q, axis=1)


def _sample_scores(q_col, cache_t, layer, page_table):
    _, _, n_heads, dh, page = cache_t.shape
    b, n_pages = page_table.shape
    n_in = PAGES_PER_STEP if n_pages % PAGES_PER_STEP == 0 else 1

    def page_spec(u):
        return pl.BlockSpec((1, 1, n_heads, dh, page),
                            lambda bi, g, pt: (layer, pt[bi * n_pages + g * n_in + u], 0, 0, 0))

    return pl.pallas_call(
        _sample_scores_kernel,
        grid_spec=pltpu.PrefetchScalarGridSpec(
            num_scalar_prefetch=1, grid=(b, n_pages // n_in),
            in_specs=[pl.BlockSpec((1, n_heads, dh, 1), lambda bi, g, pt: (bi, 0, 0, 0))]
            + [page_spec(u) for u in range(n_in)],
            out_specs=pl.BlockSpec((1, n_in, n_heads, page), lambda bi, g, pt: (bi, g, 0, 0))),
        out_shape=jax.ShapeDtypeStruct((b, n_pages, n_heads, page), F32),
        compiler_params=_params("parallel", "arbitrary"), name="moba_sample_scores",
    )(page_table.reshape(-1), q_col, *([cache_t] * n_in))


def _sample_select_kernel(s_ref, idx_ref, *, per_blk):
    s = s_ref[0]
    n_pages, n_heads, _ = s.shape
    nb = n_pages // per_blk
    page_sum = jnp.sum(s, axis=-1, keepdims=True)
    gate = jnp.sum(page_sum.reshape(nb, per_blk, n_heads, 1), axis=1)
    blk_id = lax.broadcasted_iota(jnp.int32, gate.shape, 0)
    for r in range(idx_ref.shape[1]):
        top = jnp.max(gate, axis=0, keepdims=True)
        idx = jnp.min(jnp.where(gate == top, blk_id, nb), axis=0, keepdims=True)
        idx_ref[0, r] = idx[0]
        gate = jnp.where(blk_id == idx, NEG_INF, gate)


def _sample_select(scores, per_blk):
    b, n_pages, n_heads, page = scores.shape
    topk = min(MOBA_TOPK, n_pages // per_blk)
    return pl.pallas_call(
        functools.partial(_sample_select_kernel, per_blk=per_blk),
        grid=(b,),
        in_specs=[pl.BlockSpec((1, n_pages, n_heads, page), lambda bi: (bi, 0, 0, 0))],
        out_specs=pl.BlockSpec((1, topk, n_heads, 1), lambda bi: (bi, 0, 0, 0)),
        out_shape=jax.ShapeDtypeStruct((b, topk, n_heads, 1), jnp.int32),
        compiler_params=_params("parallel"), name="moba_sample_select",
    )(scores)


def _sample_attend_kernel(pg_ref, lp_ref, q_ref, kn_ref, vn_ref, s_ref, cv_hbm, o_ref, vbuf, sems,
                          *, layer, n_heads, n_sel, dh):
    bi = pl.program_id(0)
    slot = bi % 2

    def value_copies(seq, buf_slot):
        return [pltpu.make_async_copy(cv_hbm.at[layer, pg_ref[(seq * n_heads + h) * n_sel + r], h],
                                      vbuf.at[buf_slot, h, r], sems.at[buf_slot])
                for h in range(n_heads) for r in range(n_sel)]

    @pl.when(bi == 0)
    def _():
        for c in value_copies(bi, slot):
            c.start()

    @pl.when(bi + 1 < pl.num_programs(0))
    def _():
        for c in value_copies(bi + 1, 1 - slot):
            c.start()

    q = q_ref[0]
    rows, selfs = [], []
    for h in range(n_heads):
        sl = slice(h * dh, (h + 1) * dh)
        rows.append(jnp.concatenate(
            [s_ref[0, lp_ref[(bi * n_heads + h) * n_sel + r], pl.ds(h, 1), :] for r in range(n_sel)],
            axis=1))
        selfs.append(jnp.sum(q[:, sl] * kn_ref[0][:, sl], axis=-1, keepdims=True))
    s = jnp.concatenate(rows, axis=0)
    s_self = jnp.concatenate(selfs, axis=0)
    m = jnp.maximum(jnp.max(s, axis=-1, keepdims=True), s_self)
    p = jnp.exp(s - m)
    p_self = jnp.exp(s_self - m)
    inv_l = 1.0 / (jnp.sum(p, axis=-1, keepdims=True) + p_self)

    for c in value_copies(bi, slot):
        c.wait()
    pv = [_bdot_nt(jnp.broadcast_to(p[h:h + 1, :], (8, p.shape[1])),
                   jnp.concatenate([vbuf[slot, h, r] for r in range(n_sel)], axis=1))[0:1, :]
          for h in range(n_heads)]
    for h in range(n_heads):
        sl = slice(h * dh, (h + 1) * dh)
        o_ref[0, :, sl] = (pv[h] + p_self[h:h + 1, :] * vn_ref[0][:, sl]) * inv_l[h:h + 1, :]


def _sample_attend(q, k_new, v_new, scores, cache_vt, layer, phys, logical):
    b, _, d = q.shape
    _, _, n_heads, dh, page = cache_vt.shape
    n_pages = scores.shape[1]
    n_sel = phys.shape[-1]
    row_spec = pl.BlockSpec((1, 1, d), lambda bi, pg, lp: (bi, 0, 0))
    return pl.pallas_call(
        functools.partial(_sample_attend_kernel, layer=layer, n_heads=n_heads, n_sel=n_sel, dh=dh),
        grid_spec=pltpu.PrefetchScalarGridSpec(
            num_scalar_prefetch=2, grid=(b,),
            in_specs=[row_spec, row_spec, row_spec,
                      pl.BlockSpec((1, n_pages, n_heads, page), lambda bi, pg, lp: (bi, 0, 0, 0)),
                      pl.BlockSpec(memory_space=pl.ANY)],
            out_specs=row_spec,
            scratch_shapes=[pltpu.VMEM((2, n_heads, n_sel, dh, page), F32),
                            pltpu.SemaphoreType.DMA((2,))]),
        out_shape=jax.ShapeDtypeStruct((b, 1, d), F32),
        compiler_params=_params("arbitrary"), name="moba_sample_attend",
    )(phys.reshape(-1), logical.reshape(-1), q, k_new, v_new, scores, cache_vt)


def _matmul_res_kernel(a_ref, w_ref, r_ref, o_ref):
    o_ref[...] = r_ref[...] + _bdot(a_ref[...], w_ref[...])


def _matmul_res(a, w_bf, res, *, tm):
    m, k = a.shape
    n = w_bf.shape[1]
    return pl.pallas_call(
        _matmul_res_kernel, grid=(m // tm,),
        in_specs=[pl.BlockSpec((tm, k), lambda i: (i, 0)), pl.BlockSpec((k, n), lambda i: (0, 0)),
                  pl.BlockSpec((tm, n), lambda i: (i, 0))],
        out_specs=pl.BlockSpec((tm, n), lambda i: (i, 0)),
        out_shape=jax.ShapeDtypeStruct((m, n), F32),
        compiler_params=_params("parallel"), name="matmul_residual",
    )(a, w_bf, res)


def _ffn_kernel(x_ref, nw_ref, wg_ref, wu_ref, wo_ref, *refs, final_norm):
    if final_norm:
        fw_ref, o_ref, h_scr, acc_scr = refs
    else:
        o_ref, h_scr, acc_scr = refs
    j = pl.program_id(1)

    @pl.when(j == 0)
    def _():
        h_scr[...] = _rms(x_ref[...], nw_ref[...]).astype(BF16)
        acc_scr[...] = x_ref[...]

    h = h_scr[...]
    gate = jnp.dot(h, wg_ref[...], preferred_element_type=F32)
    up = jnp.dot(h, wu_ref[...], preferred_element_type=F32)
    acc_scr[...] += _bdot(_silu(gate) * up, wo_ref[...])

    @pl.when(j == pl.num_programs(1) - 1)
    def _():
        y = acc_scr[...]
        o_ref[...] = _rms(y, fw_ref[...]) if final_norm else y


def _ffn(x, nw, w_in_bf, w_out_bf, final_w=None, *, tm, n_hidden_tiles=2):
    m, d = x.shape
    hidden = w_out_bf.shape[0]
    th = hidden // n_hidden_tiles
    assert th * n_hidden_tiles == hidden and th % 128 == 0
    in_specs = [pl.BlockSpec((tm, d), lambda i, j: (i, 0)), pl.BlockSpec((1, d), lambda i, j: (0, 0)),
                pl.BlockSpec((d, th), lambda i, j: (0, j)),
                pl.BlockSpec((d, th), lambda i, j: (0, n_hidden_tiles + j)),
                pl.BlockSpec((th, d), lambda i, j: (j, 0))]
    args = [x, nw.reshape(1, d), w_in_bf, w_in_bf, w_out_bf]
    if final_w is not None:
        in_specs.append(pl.BlockSpec((1, d), lambda i, j: (0, 0)))
        args.append(final_w.reshape(1, d))
    return pl.pallas_call(
        functools.partial(_ffn_kernel, final_norm=final_w is not None),
        grid=(m // tm, n_hidden_tiles), in_specs=in_specs,
        out_specs=pl.BlockSpec((tm, d), lambda i, j: (i, 0)),
        out_shape=jax.ShapeDtypeStruct((m, d), F32),
        scratch_shapes=[pltpu.VMEM((tm, d), BF16), pltpu.VMEM((tm, d), F32)],
        compiler_params=_params("parallel", "arbitrary"), name="swiglu_ffn",
    )(*args)


def _norm_matmul_kernel(x_ref, nw_ref, w_ref, o_ref, h_scr):
    @pl.when(pl.program_id(1) == 0)
    def _():
        h_scr[...] = _rms(x_ref[...], nw_ref[...]).astype(BF16)

    o_ref[...] = jnp.dot(h_scr[...], w_ref[...], preferred_element_type=F32)


def _norm_matmul(x, nw, w_bf, *, tm, tn):
    m, d = x.shape
    n = w_bf.shape[1]
    return pl.pallas_call(
        _norm_matmul_kernel, grid=(m // tm, n // tn),
        in_specs=[pl.BlockSpec((tm, d), lambda i, j: (i, 0)), pl.BlockSpec((1, d), lambda i, j: (0, 0)),
                  pl.BlockSpec((d, tn), lambda i, j: (0, j))],
        out_specs=pl.BlockSpec((tm, tn), lambda i, j: (i, j)),
        out_shape=jax.ShapeDtypeStruct((m, n), F32),
        scratch_shapes=[pltpu.VMEM((tm, d), BF16)],
        compiler_params=_params("parallel", "arbitrary"), name="norm_matmul",
    )(x, nw.reshape(1, d), w_bf)


def _gates_kernel(x_ref, nw_ref, w_ref, alog_ref, dtb_ref, beta_ref, g_ref, gc_ref, *, nh, chunk):
    h = _rms(x_ref[...], nw_ref[...])
    ba = jnp.dot(h, w_ref[...], precision=HI, preferred_element_type=F32)
    b_in, a_in = ba[:, 0:nh], ba[:, LANES:LANES + nh]
    z = a_in + dtb_ref[...]
    softplus = jnp.maximum(z, 0.0) + jnp.log1p(jnp.exp(-jnp.abs(z)))
    g = -jnp.exp(alog_ref[...]) * softplus
    beta_ref[...] = jax.nn.sigmoid(b_in)
    g_ref[...] = g
    tm = g.shape[0]
    if chunk == 1:
        gc_ref[...] = g
    else:
        r = lax.broadcasted_iota(jnp.int32, (chunk, chunk), 0)
        c = lax.broadcasted_iota(jnp.int32, (chunk, chunk), 1)
        tri = jnp.where(r >= c, 1.0, 0.0)
        for i in range(tm // chunk):
            gc_ref[i * chunk:(i + 1) * chunk, :] = jnp.dot(
                tri, g[i * chunk:(i + 1) * chunk, :], precision=HI, preferred_element_type=F32)


def _gates(x, nw, w_ba, a_log, dt_bias, *, tm, chunk):
    m, d = x.shape
    nh = a_log.shape[0]
    spec = pl.BlockSpec((tm, nh), lambda i: (i, 0))
    small = pl.BlockSpec((1, nh), lambda i: (0, 0))
    return pl.pallas_call(
        functools.partial(_gates_kernel, nh=nh, chunk=chunk), grid=(m // tm,),
        in_specs=[pl.BlockSpec((tm, d), lambda i: (i, 0)), pl.BlockSpec((1, d), lambda i: (0, 0)),
                  pl.BlockSpec(w_ba.shape, lambda i: (0, 0)), small, small],
        out_specs=[spec, spec, spec],
        out_shape=[jax.ShapeDtypeStruct((m, nh), F32)] * 3,
        compiler_params=_params("parallel"), name="delta_gates",
    )(x, nw.reshape(1, d), w_ba, a_log.reshape(1, nh), dt_bias.reshape(1, nh))


def _post_conv(conv, q_ref, k_ref, v_ref, *, key_dim, dk):
    y = _silu(conv)
    for j in range(key_dim // dk):
        for ref, off, scale in ((q_ref, 0, dk ** -0.5), (k_ref, key_dim, 1.0)):
            t = y[:, off + j * dk:off + (j + 1) * dk]
            inv = lax.rsqrt(jnp.sum(t * t, axis=-1, keepdims=True) + NORM_EPS)
            ref[0, :, j * dk:(j + 1) * dk] = t * (inv * scale)
    v_ref[0] = y[:, 2 * key_dim:]


SUBLANES = 8


def _delta_inproj_kernel(x_ref, nw_ref, w_ref, cw_ref, q_ref, k_ref, v_ref, z_ref, tail_ref, halo,
                         *, key_dim, dk, tiles_per_seq):
    i = pl.program_id(0)
    width, conv_dim = cw_ref.shape
    tm = x_ref.shape[0]

    @pl.when(i % tiles_per_seq == 0)
    def _():
        halo[...] = jnp.zeros_like(halo)

    h = _rms(x_ref[...], nw_ref[...]).astype(BF16)
    row = lax.broadcasted_iota(jnp.int32, (SUBLANES, key_dim), 0)
    for j in range(w_ref.shape[1] // key_dim):
        cols = slice(j * key_dim, (j + 1) * key_dim)
        acc = jnp.dot(h, w_ref[:, cols], preferred_element_type=F32)
        if j * key_dim >= conv_dim:
            z_ref[:, j * key_dim - conv_dim:(j + 1) * key_dim - conv_dim] = acc
            continue
        prev = halo[:, cols]
        conv = acc * cw_ref[width - 1:width, cols]
        for s in range(1, width):
            rolled = pltpu.roll(acc, s, 0)
            head = jnp.where(row < s, pltpu.roll(prev, s, 0), rolled[0:SUBLANES])
            conv = conv + jnp.concatenate([head, rolled[SUBLANES:]], axis=0) * cw_ref[width - 1 - s:width - s, cols]
        halo[:, cols] = acc[tm - SUBLANES:tm]
        tail_ref[0, :, cols] = acc[tm - SUBLANES:tm]
        y = _silu(conv)
        if j >= 2:
            v_ref[:, (j - 2) * key_dim:(j - 1) * key_dim] = y
            continue
        ref, scale = (q_ref, dk ** -0.5) if j == 0 else (k_ref, 1.0)
        for hd in range(key_dim // dk):
            t = y[:, hd * dk:(hd + 1) * dk]
            inv = lax.rsqrt(jnp.sum(t * t, axis=-1, keepdims=True) + NORM_EPS)
            ref[:, hd * dk:(hd + 1) * dk] = t * (inv * scale)


def _delta_inproj(x, nw, w_bf, conv_w, *, seq_len, key_dim, dk, tm):
    m, d = x.shape
    width, conv_dim = conv_w.shape
    n = w_bf.shape[1]
    val_dim = conv_dim - 2 * key_dim
    assert n == conv_dim + val_dim and val_dim % key_dim == 0 and seq_len % tm == 0 and tm > SUBLANES
    tiles_per_seq = seq_len // tm
    rows = lambda c: pl.BlockSpec((tm, c), lambda i: (i, 0))
    return pl.pallas_call(
        functools.partial(_delta_inproj_kernel, key_dim=key_dim, dk=dk, tiles_per_seq=tiles_per_seq),
        grid=(m // tm,),
        in_specs=[rows(d), pl.BlockSpec((1, d), lambda i: (0, 0)),
                  pl.BlockSpec((d, n), lambda i: (0, 0), pipeline_mode=pl.Buffered(1)),
                  pl.BlockSpec((width, conv_dim), lambda i: (0, 0))],
        out_specs=[rows(key_dim), rows(key_dim), rows(val_dim), rows(val_dim),
                   pl.BlockSpec((1, SUBLANES, conv_dim), lambda i: (i // tiles_per_seq, 0, 0))],
        out_shape=[jax.ShapeDtypeStruct((m, key_dim), F32), jax.ShapeDtypeStruct((m, key_dim), F32),
                   jax.ShapeDtypeStruct((m, val_dim), F32), jax.ShapeDtypeStruct((m, val_dim), F32),
                   jax.ShapeDtypeStruct((m // seq_len, SUBLANES, conv_dim), F32)],
        scratch_shapes=[pltpu.VMEM((SUBLANES, conv_dim), F32)],
        compiler_params=_params("arbitrary"), name="delta_inproj_conv",
    )(x, nw.reshape(1, d), w_bf, conv_w)


def _conv_step_kernel(hist_ref, cur_ref, w_ref, q_ref, k_ref, v_ref, *, key_dim, dk, width):
    conv = cur_ref[...] * w_ref[width - 1:width, :]
    for i in range(width - 1):
        conv = conv + hist_ref[i] * w_ref[i:i + 1, :]
    _post_conv(conv, q_ref, k_ref, v_ref, key_dim=key_dim, dk=dk)


def _conv_step(hist_t, cur, conv_w, *, key_dim, dk):
    _, b, c = hist_t.shape
    width = conv_w.shape[0]
    val_dim = c - 2 * key_dim
    return pl.pallas_call(
        functools.partial(_conv_step_kernel, key_dim=key_dim, dk=dk, width=width),
        out_shape=[jax.ShapeDtypeStruct((1, b, key_dim), F32), jax.ShapeDtypeStruct((1, b, key_dim), F32),
                   jax.ShapeDtypeStruct((1, b, val_dim), F32)],
        compiler_params=pltpu.CompilerParams(vmem_limit_bytes=VMEM_LIMIT_BYTES), name="delta_conv_step",
    )(hist_t, cur, conv_w)


def _unit_lower_inverse(mats, row, col):
    c = mats[0].shape[0]
    shift = INV_BASE.bit_length() - 1
    same = lax.shift_right_logical(row, shift) == lax.shift_right_logical(col, shift)
    eye = jnp.where(row == col, 1.0, 0.0)
    power = [jnp.where(same, a, 0.0) for a in mats]
    xs = [eye - p for p in power]
    for _ in range(shift - 1):
        power = [_bdot(p, p) for p in power]
        xs = [x + _bdot(x, p) for x, p in zip(xs, power)]
    size = INV_BASE
    while size < c:
        shift += 1
        wider = lax.shift_right_logical(row, shift) == lax.shift_right_logical(col, shift)
        off = wider & jnp.logical_not(same)
        ex = [_bdot(jnp.where(off, a, 0.0), x) for a, x in zip(mats, xs)]
        xs = [x - _bdot(x, e) for x, e in zip(xs, ex)]
        same = wider
        size *= 2
    return xs


def _delta_chunk_kernel(q_ref, k_ref, v_ref, beta_ref, g_ref, gt_ref, s0_ref,
                        o_ref, s_ref, state, *, chunk, dk, dv):
    ci = pl.program_id(1)

    @pl.when(ci == 0)
    def _():
        state[...] = s0_ref[...]

    row = lax.broadcasted_iota(jnp.int32, (chunk, chunk), 0)
    col = lax.broadcasted_iota(jnp.int32, (chunk, chunk), 1)
    causal = row >= col
    n_seq, nh = state.shape[:2]
    hk = q_ref.shape[2] // dk
    rep = nh // hk
    chains = [(s, h) for s in range(n_seq) for h in range(nh)]
    groups = [(s, j) for s in range(n_seq) for j in range(hk)]
    q = {(s, j): q_ref[s, :, j * dk:(j + 1) * dk] for s, j in groups}
    k = {(s, j): k_ref[s, :, j * dk:(j + 1) * dk] for s, j in groups}
    kt = {c: k[c].T for c in groups}
    kk = {c: _bdot(k[c], kt[c]) for c in groups}
    qk = {c: _bdot(q[c], kt[c]) for c in groups}
    beta_all = [beta_ref[s] for s in range(n_seq)]
    g_all = [g_ref[s] for s in range(n_seq)]
    gt_all = [gt_ref[s, 0] for s in range(n_seq)]
    beta = {(s, h): beta_all[s][:, h:h + 1] for s, h in chains}
    gcol = {(s, h): g_all[s][:, h:h + 1] for s, h in chains}
    grow = {(s, h): gt_all[s][h:h + 1, :] for s, h in chains}
    decay = {c: jnp.where(causal, jnp.exp(jnp.where(causal, gcol[c] - grow[c], 0.0)), 0.0)
             for c in chains}
    a = [jnp.where(row > col, beta[s, h] * kk[s, h // rep] * decay[s, h], 0.0) for s, h in chains]
    t_inv = _unit_lower_inverse(a, row, col)
    t_off = {c: jnp.where(row == col, 0.0, t) for c, t in zip(chains, t_inv)}
    eg = {c: jnp.exp(gcol[c]) for c in chains}
    bv = {(s, h): beta[s, h] * v_ref[s, :, h * dv:(h + 1) * dv] for s, h in chains}
    bk = {(s, h): (beta[s, h] * eg[s, h]) * k[s, h // rep] for s, h in chains}
    u_val = {c: bv[c] + _bdot(t_off[c], bv[c]) for c in chains}
    w = {c: bk[c] + _bdot(t_off[c], bk[c]) for c in chains}
    s_old = {(s, h): state[s, h] for s, h in chains}
    u = {c: u_val[c] - _bdot(w[c], s_old[c]) for c in chains}
    o_state = {(s, h): _bdot(q[s, h // rep] * eg[s, h], s_old[s, h]) for s, h in chains}
    o_local = {(s, h): _bdot(qk[s, h // rep] * decay[s, h], u[s, h]) for s, h in chains}
    g_last = {c: gcol[c][chunk - 1:chunk, :] for c in chains}
    s_add = {(s, h): _bdot(kt[s, h // rep] * jnp.exp(g_last[s, h] - grow[s, h]), u[s, h])
             for s, h in chains}
    for s, h in chains:
        o_ref[s, :, h * dv:(h + 1) * dv] = o_state[s, h] + o_local[s, h]
        state[s, h] = s_old[s, h] * jnp.exp(g_last[s, h]) + s_add[s, h]

    @pl.when(ci == pl.num_programs(1) - 1)
    def _():
        s_ref[...] = state[...]


DELTA_SEQS_PER_STEP = 2


def _delta_chunked(qn, kn, v, beta, gc, s0, *, dk, dv, chunk):
    b, t, key_dim = qn.shape
    nh = beta.shape[-1]
    nc = t // chunk
    ns = DELTA_SEQS_PER_STEP if b % DELTA_SEQS_PER_STEP == 0 else 1
    gt = gc.reshape(b, nc, chunk, nh).transpose(0, 1, 3, 2)
    tok = lambda n: pl.BlockSpec((ns, chunk, n), lambda bi, c: (bi, c, 0))
    st = pl.BlockSpec((ns, nh, dk, dv), lambda bi, c: (bi, 0, 0, 0))
    return pl.pallas_call(
        functools.partial(_delta_chunk_kernel, chunk=chunk, dk=dk, dv=dv),
        grid=(b // ns, nc),
        in_specs=[tok(key_dim), tok(key_dim), tok(nh * dv), tok(nh), tok(nh),
                  pl.BlockSpec((ns, 1, nh, chunk), lambda bi, c: (bi, c, 0, 0)), st],
        out_specs=[tok(nh * dv), st],
        out_shape=[jax.ShapeDtypeStruct((b, t, nh * dv), F32),
                   jax.ShapeDtypeStruct((b, nh, dk, dv), F32)],
        scratch_shapes=[pltpu.VMEM((ns, nh, dk, dv), F32)],
        compiler_params=_params("parallel", "arbitrary"), name="delta_chunked",
    )(qn, kn, v, beta, gc, gt, s0)


def _delta_step_kernel(q_ref, k_ref, qc_ref, kc_ref, v_ref, beta_ref, g_ref, s0_ref, o_ref, s_ref,
                       *, dk, dv, rep):
    nh = s0_ref.shape[1]
    for hv in range(nh):
        j = hv // rep
        q_row = q_ref[0][:, j * dk:(j + 1) * dk]
        k_row = k_ref[0][:, j * dk:(j + 1) * dk]
        q_col = qc_ref[0, j * dk:(j + 1) * dk, :]
        k_col = kc_ref[0, j * dk:(j + 1) * dk, :]
        beta = beta_ref[0][:, hv:hv + 1]
        eg = jnp.exp(g_ref[0][:, hv:hv + 1])
        s_old = s0_ref[0, hv]
        k_s = jnp.sum(s_old * k_col, axis=0, keepdims=True)
        q_s = jnp.sum(s_old * q_col, axis=0, keepdims=True)
        u = beta * (v_ref[0][:, hv * dv:(hv + 1) * dv] - eg * k_s)
        qk = jnp.sum(q_row * k_row, axis=-1, keepdims=True)
        o_ref[0, :, hv * dv:(hv + 1) * dv] = eg * q_s + qk * u
        s_ref[0, hv] = s_old * eg + k_col * u


def _delta_step(qn, kn, v, beta, g, s0, *, dk, dv):
    b, _, key_dim = qn.shape
    nh = beta.shape[-1]
    rep = nh // (key_dim // dk)
    row = lambda n: pl.BlockSpec((1, 1, n), lambda bi: (bi, 0, 0))
    col = pl.BlockSpec((1, key_dim, 1), lambda bi: (bi, 0, 0))
    st = pl.BlockSpec((1, nh, dk, dv), lambda bi: (bi, 0, 0, 0))
    return pl.pallas_call(
        functools.partial(_delta_step_kernel, dk=dk, dv=dv, rep=rep), grid=(b,),
        in_specs=[row(key_dim), row(key_dim), col, col, row(nh * dv), row(nh), row(nh), st],
        out_specs=[row(nh * dv), st],
        out_shape=[jax.ShapeDtypeStruct((b, 1, nh * dv), F32), jax.ShapeDtypeStruct(s0.shape, F32)],
        compiler_params=_params("parallel"), name="delta_step",
    )(qn, kn, qn.reshape(b, key_dim, 1), kn.reshape(b, key_dim, 1), v, beta, g, s0)


def _gated_out_kernel(o_ref, z_ref, gw_ref, w_ref, r_ref, y_ref, a_scr, *, dv):
    gw = gw_ref[...]
    for hv in range(o_ref.shape[1] // dv):
        sl = slice(hv * dv, (hv + 1) * dv)
        o = o_ref[:, sl]
        inv = lax.rsqrt(jnp.mean(o * o, axis=-1, keepdims=True) + NORM_EPS)
        a_scr[:, sl] = (o * inv * gw * _silu(z_ref[:, sl])).astype(BF16)
    y_ref[...] = r_ref[...] + jnp.dot(a_scr[...], w_ref[...], preferred_element_type=F32)


def _gated_out(o, proj, z_col_block, gnorm_w, w_out_bf, res, *, tm, dv):
    m, val_dim = o.shape
    n = w_out_bf.shape[1]
    return pl.pallas_call(
        functools.partial(_gated_out_kernel, dv=dv), grid=(m // tm,),
        in_specs=[pl.BlockSpec((tm, val_dim), lambda i: (i, 0)),
                  pl.BlockSpec((tm, val_dim), lambda i: (i, z_col_block)),
                  pl.BlockSpec((1, dv), lambda i: (0, 0)),
                  pl.BlockSpec((val_dim, n), lambda i: (0, 0)),
                  pl.BlockSpec((tm, n), lambda i: (i, 0))],
        out_specs=pl.BlockSpec((tm, n), lambda i: (i, 0)),
        out_shape=jax.ShapeDtypeStruct((m, n), F32),
        scratch_shapes=[pltpu.VMEM((tm, val_dim), BF16)],
        compiler_params=_params("parallel"), name="delta_gated_out",
    )(o, proj, gnorm_w.reshape(1, dv), w_out_bf, res)


def kernel(x_prompt, x_sample, cache_k, cache_v, page_table, state_conv, state_ssm, norm_mix, w_qkv_attn, w_o_attn, w_in_delta, conv_w_delta, a_log_delta, dt_bias_delta, gnorm_delta, w_out_delta, norm_ffn, w_ffn_in, w_ffn_out, norm_final):
    b, t, d = x_prompt.shape
    sb = x_sample.shape[0]
    assert x_sample.shape[1] == 1, "the sample group advances one token per sequence"
    n_heads, head_dim = cache_k.shape[-2:]
    page = cache_k.shape[2]
    n_pages = page_table.shape[1]
    past_len = n_pages * page
    assert past_len % MOBA_BLOCK == 0 and MOBA_BLOCK % page == 0 and t % MOBA_BLOCK == 0
    depth = norm_mix.shape[0]
    assert depth == 2, "layer 0 is attention, layer 1 is the gated DeltaNet"
    nh_v, dk, dv = state_ssm.shape[2:]
    conv_dim = conv_w_delta.shape[-1]
    val_dim = nh_v * dv
    key_dim = (conv_dim - val_dim) // 2
    assert conv_dim % val_dim == 0 and t % DN_CHUNK == 0

    xp = x_prompt.reshape(b * t, d)
    xs = x_sample.reshape(sb, d)
    tm_p = _row_tile(b * t, 512)

    w_qkv = w_qkv_attn[0].astype(BF16)
    w_o = w_o_attn[0].astype(BF16)
    nb_p = t // MOBA_BLOCK
    q_p, k_p, v_p, kb_p, vt_p, km_p = _qkv_rope(
        xp, norm_mix[0], w_qkv, jnp.arange(t, dtype=jnp.int32), tm=MOBA_BLOCK, head_dim=head_dim,
        q_scale=head_dim ** -0.5 * math.log2(math.e), q_dtype=BF16, block_outputs=True)
    o_p = _moba_prompt(q_p.reshape(b, t, d), kb_p.reshape(b, t, d), vt_p.reshape(b, nb_p, d, MOBA_BLOCK),
                       km_p.reshape(b, nb_p, d), head_dim=head_dim)
    xp = _matmul_res(o_p.reshape(b * t, d), w_o, xp, tm=tm_p)

    q_s, k_s, v_s = _qkv_rope(
        xs, norm_mix[0], w_qkv, jnp.full((sb,), past_len, jnp.int32), tm=sb, head_dim=head_dim,
        q_scale=head_dim ** -0.5, q_dtype=F32, block_outputs=False)
    per_blk = MOBA_BLOCK // page
    cache_kt = cache_k.transpose(0, 1, 3, 4, 2)
    cache_vt = cache_v.transpose(0, 1, 3, 4, 2)
    scores = _sample_scores(q_s.reshape(sb, n_heads, head_dim, 1), cache_kt, 0, page_table)
    sel = _sample_select(scores, per_blk)[..., 0]
    logical = (sel.transpose(0, 2, 1)[..., None] * per_blk
               + jnp.arange(per_blk, dtype=jnp.int32)).reshape(sb, n_heads, -1)
    phys = jnp.take_along_axis(page_table, logical.reshape(sb, -1), axis=1).reshape(sb, n_heads, -1)
    o_s = _sample_attend(q_s.reshape(sb, 1, d), k_s.reshape(sb, 1, d), v_s.reshape(sb, 1, d),
                         scores, cache_vt, 0, phys, logical)
    xs = _matmul_res(o_s.reshape(sb, d), w_o, xs, tm=sb)

    xp = _ffn(xp, norm_ffn[0], w_ffn_in[0].astype(BF16), w_ffn_out[0].astype(BF16), tm=tm_p)
    xs = _ffn(xs, norm_ffn[0], w_ffn_in[0].astype(BF16), w_ffn_out[0].astype(BF16), tm=sb)

    w_in = w_in_delta[0]
    main = conv_dim + val_dim
    w_main = w_in[:, :main].astype(BF16)
    w_ba = jnp.zeros((d, 2 * LANES), F32).at[:, :nh_v].set(w_in[:, main:main + nh_v])
    w_ba = w_ba.at[:, LANES:LANES + nh_v].set(w_in[:, main + nh_v:])
    w_out = w_out_delta[0].astype(BF16)
    conv_w = conv_w_delta[0]
    z_block = conv_dim // val_dim

    qn_p, kn_p, vv_p, z_p, tail_p = _delta_inproj(xp, norm_mix[1], w_main, conv_w, seq_len=t,
                                                  key_dim=key_dim, dk=dk, tm=256)
    beta_p, _, gc_p = _gates(xp, norm_mix[1], w_ba, a_log_delta[0], dt_bias_delta[0],
                             tm=tm_p, chunk=DN_CHUNK)
    o_dp, ssm_p = _delta_chunked(qn_p.reshape(b, t, key_dim), kn_p.reshape(b, t, key_dim),
                                 vv_p.reshape(b, t, val_dim), beta_p.reshape(b, t, nh_v),
                                 gc_p.reshape(b, t, nh_v), jnp.zeros((b,) + state_ssm.shape[2:], F32),
                                 dk=dk, dv=dv, chunk=DN_CHUNK)
    xp = _gated_out(o_dp.reshape(b * t, val_dim), z_p, 0, gnorm_delta[0], w_out, xp, tm=tm_p, dv=dv)
    conv_p = tail_p[:, SUBLANES - (conv_w.shape[0] - 1):, :]

    proj_s = _norm_matmul(xs, norm_mix[1], w_main, tm=sb, tn=1024)
    beta_s, g_s, _ = _gates(xs, norm_mix[1], w_ba, a_log_delta[0], dt_bias_delta[0], tm=sb, chunk=1)
    hist_t = state_conv[0].transpose(1, 0, 2)
    cur_s = proj_s[:, :conv_dim]
    qn_s, kn_s, vv_s = _conv_step(hist_t, cur_s, conv_w, key_dim=key_dim, dk=dk)
    o_ds, ssm_s = _delta_step(qn_s.reshape(sb, 1, key_dim), kn_s.reshape(sb, 1, key_dim),
                              vv_s.reshape(sb, 1, val_dim), beta_s.reshape(sb, 1, nh_v),
                              g_s.reshape(sb, 1, nh_v), state_ssm[0], dk=dk, dv=dv)
    xs = _gated_out(o_ds.reshape(sb, val_dim), proj_s, z_block, gnorm_delta[0], w_out, xs, tm=sb, dv=dv)
    conv_s = jnp.concatenate([state_conv[0][:, 1:], cur_s[:, None, :]], axis=1)

    y_p = _ffn(xp, norm_ffn[1], w_ffn_in[1].astype(BF16), w_ffn_out[1].astype(BF16), norm_final, tm=tm_p)
    y_s = _ffn(xs, norm_ffn[1], w_ffn_in[1].astype(BF16), w_ffn_out[1].astype(BF16), norm_final, tm=sb)

    kv_p = (b, t, n_heads, head_dim)
    kv_s = (sb, 1, n_heads, head_dim)
    return (y_p.reshape(b, t, d), y_s.reshape(sb, 1, d),
            k_p.reshape(kv_p)[None], v_p.reshape(kv_p)[None],
            k_s.reshape(kv_s)[None], v_s.reshape(kv_s)[None],
            conv_p[None], ssm_p[None], conv_s[None], ssm_s[None])
```

```python
import functools
import math

import jax
import jax.numpy as jnp
from jax import lax
from jax.experimental import pallas as pl
from jax.experimental.pallas import tpu as pltpu

F32 = jnp.float32
BF16 = jnp.bfloat16

NORM_EPS = 1e-6
ROPE_THETA = 10000.0
MOBA_BLOCK = 256
MOBA_TOPK = 3
DN_CHUNK = 64
KV_GROUP = 2
ONES_ROWS = 16
MOBA_HEADS_PER_STEP = 4
LANES = 128
INV_BASE = 8
VMEM_LIMIT_BYTES = 48 * 1024 * 1024
NEG_INF = float("-inf")
HI = lax.Precision.HIGHEST


def _params(*sem):
    return pltpu.CompilerParams(dimension_semantics=sem, vmem_limit_bytes=VMEM_LIMIT_BYTES)


def _rms(x, w):
    ms = jnp.mean(x * x, axis=-1, keepdims=True)
    return x * lax.rsqrt(ms + NORM_EPS) * w


def _bdot(a, b):
    return jnp.dot(a.astype(BF16), b.astype(BF16), preferred_element_type=F32)


def _bdot_nt(a, b):
    return lax.dot_general(a.astype(BF16), b.astype(BF16), (((1,), (1,)), ((), ())),
                           preferred_element_type=F32)


def _silu(x):
    return x * jax.nn.sigmoid(x)


def _row_tile(m, want):
    return want if m % want == 0 else m


def _qkv_rope_kernel(x_ref, nw_ref, w_ref, cos_ref, sa_ref, sb_ref,
                     q_ref, k_ref, v_ref, *block_refs, d, half, q_scale):
    h = _rms(x_ref[...], nw_ref[...]).astype(BF16)
    rep = d // cos_ref.shape[-1]
    cos = jnp.concatenate([cos_ref[...]] * rep, axis=1)
    sa = jnp.concatenate([sa_ref[...]] * rep, axis=1)
    sb = jnp.concatenate([sb_ref[...]] * rep, axis=1)

    def rope(t):
        return t * cos + pltpu.roll(t, d - half, 1) * sa + pltpu.roll(t, half, 1) * sb

    q = rope(jnp.dot(h, w_ref[:, 0:d], preferred_element_type=F32))
    k = rope(jnp.dot(h, w_ref[:, d:2 * d], preferred_element_type=F32))
    v = jnp.dot(h, w_ref[:, 2 * d:3 * d], preferred_element_type=F32)
    q_ref[...] = (q * q_scale).astype(q_ref.dtype)
    k_ref[...] = k
    v_ref[...] = v
    if block_refs:
        kb_ref, vt_ref, km_ref = block_refs
        kb_ref[...] = k.astype(BF16)
        vt_ref[0] = v.T.astype(BF16)
        km_ref[0] = jnp.mean(k, axis=0, keepdims=True)


def _rope_tables(pos, head_dim, lanes):
    half = head_dim // 2
    inv_freq = ROPE_THETA ** (-jnp.arange(half, dtype=F32) / half)
    ang = pos.astype(F32)[:, None] * inv_freq[None, :]
    reps = lanes // half
    cos = jnp.tile(jnp.cos(ang), (1, reps))
    sin = jnp.tile(jnp.sin(ang), (1, reps))
    first = (jnp.arange(lanes) % head_dim) < half
    return cos, jnp.where(first, -sin, 0.0), jnp.where(first, 0.0, sin)


def _qkv_rope(x, nw, w_bf, pos, *, tm, head_dim, q_scale, q_dtype, block_outputs):
    m, d = x.shape
    n_pos = pos.shape[0]
    lanes = 2 * head_dim
    cos, sa, sb = _rope_tables(pos, head_dim, lanes)
    pos_tiles = n_pos // tm
    tab_spec = pl.BlockSpec((tm, lanes), lambda i: (i % pos_tiles, 0))
    row_spec = pl.BlockSpec((tm, d), lambda i: (i, 0))
    out_shape = [jax.ShapeDtypeStruct((m, d), q_dtype), jax.ShapeDtypeStruct((m, d), F32),
                 jax.ShapeDtypeStruct((m, d), F32)]
    out_specs = [row_spec] * 3
    if block_outputs:
        assert tm == MOBA_BLOCK
        out_shape += [jax.ShapeDtypeStruct((m, d), BF16), jax.ShapeDtypeStruct((m // tm, d, tm), BF16),
                      jax.ShapeDtypeStruct((m // tm, 1, d), F32)]
        out_specs += [row_spec, pl.BlockSpec((1, d, tm), lambda i: (i, 0, 0)),
                      pl.BlockSpec((1, 1, d), lambda i: (i, 0, 0))]
    return pl.pallas_call(
        functools.partial(_qkv_rope_kernel, d=d, half=head_dim // 2, q_scale=q_scale),
        grid=(m // tm,),
        in_specs=[row_spec, pl.BlockSpec((1, d), lambda i: (0, 0)),
                  pl.BlockSpec((d, 3 * d), lambda i: (0, 0)), tab_spec, tab_spec, tab_spec],
        out_specs=out_specs, out_shape=out_shape,
        compiler_params=_params("parallel"), name="qkv_rope",
    )(x, nw.reshape(1, d), w_bf, cos, sa, sb)


def _select_blocks(gate, n_past, nb, width):
    blk_id = lax.broadcasted_iota(jnp.int32, (nb, width), 0)
    past = blk_id < n_past
    g = jnp.where(past, gate, NEG_INF)
    sel = jnp.zeros((nb, width), F32)
    for _ in range(min(MOBA_TOPK, nb)):
        top = jnp.max(g, axis=0, keepdims=True)
        idx = jnp.min(jnp.where(g == top, blk_id, nb), axis=0, keepdims=True)
        hit = blk_id == idx
        sel = jnp.where(hit, 1.0, sel)
        g = jnp.where(hit, NEG_INF, g)
    return jnp.where(past, sel, 0.0)


def _moba_prompt_kernel(q_ref, k_ref, vt_ref, km_ref, o_ref, sel_ref, p_ref, *, blk, dh, group):
    qi = pl.program_id(2)
    heads = range(q_ref.shape[2] // dh)
    per_tile = LANES // dh
    tile_of = lambda e: slice((e // per_tile) * LANES, (e // per_tile + 1) * LANES)
    nb = km_ref.shape[1]
    lane = lax.broadcasted_iota(jnp.int32, (blk, LANES), 1)
    qs = []
    for e in heads:
        qt = q_ref[0, :, tile_of(e)]
        lo = (e % per_tile) * dh
        qs.append(jnp.where((lane >= lo) & (lane < lo + dh), qt, jnp.zeros_like(qt)))

    def keys(j):
        rows = pl.ds(pl.multiple_of(j * blk, blk), blk)
        return [k_ref[0, rows, tile_of(e)] for e in heads]

    gates = [_bdot_nt(km_ref[0, :, tile_of(e)].astype(BF16), qs[e]) for e in heads]
    for e in heads:
        sel_ref[e] = _select_blocks(gates[e], qi, nb, blk)

    kd = keys(qi)
    vtd = vt_ref[0, qi]
    key_i = lax.broadcasted_iota(jnp.int32, (blk, blk), 0)
    qry_i = lax.broadcasted_iota(jnp.int32, (blk, blk), 1)
    ones_rows = jnp.ones((ONES_ROWS, blk), BF16)

    def values_and_sum(vt, e):
        return jnp.concatenate([vt[e * dh:(e + 1) * dh, :], ones_rows], axis=0)

    sd = [jnp.where(key_i <= qry_i, _bdot_nt(kd[e], qs[e]), NEG_INF) for e in heads]
    md = [jnp.max(s, axis=0, keepdims=True) for s in sd]
    pd = [jnp.exp2((s - m).astype(BF16)) for s, m in zip(sd, md)]
    carry = []
    for e in heads:
        carry += [md[e], jnp.dot(values_and_sum(vtd, e), pd[e], preferred_element_type=F32)]


    @pl.when(qi == 0)
    def _():
        p_ref[...] = jnp.zeros_like(p_ref)

    def pending_values(g_prev):
        j0 = jnp.maximum(g_prev, 0) * group
        live = jnp.where(g_prev >= 0, 1.0, 0.0)
        vts = [vt_ref[0, j0 + u] for u in range(group)]
        pv = [[jnp.dot(values_and_sum(vts[u], e), p_ref[e, u], preferred_element_type=F32)
               for u in range(group)] for e in heads]
        tails = []
        for e in heads:
            tail = jnp.zeros((dh + ONES_ROWS, blk), F32)
            for u in range(group):
                on = sel_ref[e, pl.ds(j0 + u, 1), :] * live > 0.0
                tail = tail + jnp.where(on, pv[e][u], 0.0)
            tails.append(tail)
        return tails

    def attend(g, c):
        j0 = g * group
        ks = [keys(j0 + u) for u in range(group)]
        ss = [[_bdot_nt(ks[u][e], qs[e]) for u in range(group)] for e in heads]
        tails = pending_values(g - 1)
        out = []
        for e in heads:
            m, acc = c[2 * e], c[2 * e + 1]
            m_new = m
            for u in range(group):
                col_max = jnp.max(ss[e][u], axis=0, keepdims=True)
                m_new = jnp.maximum(m_new, jnp.where(sel_ref[e, pl.ds(j0 + u, 1), :] > 0.0, col_max, NEG_INF))
            for u in range(group):
                p_ref[e, u] = jnp.exp2((ss[e][u] - m_new).astype(BF16))
            out += [m_new, jnp.exp2(m - m_new) * (acc + tails[e])]
        return tuple(out)

    n_groups = (qi + group - 1) // group
    c = lax.fori_loop(0, n_groups, attend, tuple(carry))
    tails = pending_values(n_groups - 1)
    outs = []
    for e in heads:
        acc = c[2 * e + 1] + tails[e]
        outs.append(acc[0:dh, :] / acc[dh:dh + 1, :])
    tiles = [jnp.concatenate(outs[i:i + per_tile], axis=0).T for i in range(0, len(outs), per_tile)]
    o_ref[0] = jnp.concatenate(tiles, axis=1).astype(o_ref.dtype)


def _moba_prompt(q_bf, k_bf, vt, kmean, *, head_dim):
    b, t, d = q_bf.shape
    blk = MOBA_BLOCK
    assert t % blk == 0
    nb = t // blk
    nhs = MOBA_HEADS_PER_STEP
    width = nhs * head_dim
    assert d % width == 0 and width % 128 == 0
    group = KV_GROUP if nb % KV_GROUP == 0 else 1
    return pl.pallas_call(
        functools.partial(_moba_prompt_kernel, blk=blk, dh=head_dim, group=group),
        grid=(b, d // width, nb),
        in_specs=[pl.BlockSpec((1, blk, width), lambda bi, hp, qi: (bi, qi, hp)),
                  pl.BlockSpec((1, t, width), lambda bi, hp, qi: (bi, 0, hp)),
                  pl.BlockSpec((1, nb, width, blk), lambda bi, hp, qi: (bi, 0, hp, 0)),
                  pl.BlockSpec((1, nb, width), lambda bi, hp, qi: (bi, 0, hp))],
        out_specs=pl.BlockSpec((1, blk, width), lambda bi, hp, qi: (bi, qi, hp)),
        out_shape=jax.ShapeDtypeStruct((b, t, d), BF16),
        scratch_shapes=[pltpu.VMEM((nhs, nb, blk), F32), pltpu.VMEM((nhs, group, blk, blk), BF16)],
        compiler_params=_params("parallel", "parallel", "arbitrary"), name="moba_prompt",
    )(q_bf, k_bf, vt, kmean)


PAGES_PER_STEP = 16


def _sample_scores_kernel(pt_ref, q_ref, *refs):
    k_refs, o_ref = refs[:-1], refs[-1]
    q = q_ref[0]
    for u, k_ref in enumerate(k_refs):
        o_ref[0, u] = jnp.sum(k_ref[0, 0] * q, axis=1)


def _sample_scores(q_col, cache_t, layer, page_table):
    _, _, n_heads, dh, page = cache_t.shape
    b, n_pages = page_table.shape
    n_in = PAGES_PER_STEP if n_pages % PAGES_PER_STEP == 0 else 1

    def page_spec(u):
        return pl.BlockSpec((1, 1, n_heads, dh, page),
                            lambda bi, g, pt: (layer, pt[bi * n_pages + g * n_in + u], 0, 0, 0))

    return pl.pallas_call(
        _sample_scores_kernel,
        grid_spec=pltpu.PrefetchScalarGridSpec(
            num_scalar_prefetch=1, grid=(b, n_pages // n_in),
            in_specs=[pl.BlockSpec((1, n_heads, dh, 1), lambda bi, g, pt: (bi, 0, 0, 0))]
            + [page_spec(u) for u in range(n_in)],
            out_specs=pl.BlockSpec((1, n_in, n_heads, page), lambda bi, g, pt: (bi, g, 0, 0))),
        out_shape=jax.ShapeDtypeStruct((b, n_pages, n_heads, page), F32),
        compiler_params=_params("parallel", "arbitrary"), name="moba_sample_scores",
    )(page_table.reshape(-1), q_col, *([cache_t] * n_in))


def _sample_select_kernel(s_ref, idx_ref, *, per_blk):
    s = s_ref[0]
    n_pages, n_heads, _ = s.shape
    nb = n_pages // per_blk
    page_sum = jnp.sum(s, axis=-1, keepdims=True)
    gate = jnp.sum(page_sum.reshape(nb, per_blk, n_heads, 1), axis=1)
    blk_id = lax.broadcasted_iota(jnp.int32, gate.shape, 0)
    for r in range(idx_ref.shape[1]):
        top = jnp.max(gate, axis=0, keepdims=True)
        idx = jnp.min(jnp.where(gate == top, blk_id, nb), axis=0, keepdims=True)
        idx_ref[0, r] = idx[0]
        gate = jnp.where(blk_id == idx, NEG_INF, gate)


def _sample_select(scores, per_blk):
    b, n_pages, n_heads, page = scores.shape
    topk = min(MOBA_TOPK, n_pages // per_blk)
    return pl.pallas_call(
        functools.partial(_sample_select_kernel, per_blk=per_blk),
        grid=(b,),
        in_specs=[pl.BlockSpec((1, n_pages, n_heads, page), lambda bi: (bi, 0, 0, 0))],
        out_specs=pl.BlockSpec((1, topk, n_heads, 1), lambda bi: (bi, 0, 0, 0)),
        out_shape=jax.ShapeDtypeStruct((b, topk, n_heads, 1), jnp.int32),
        compiler_params=_params("parallel"), name="moba_sample_select",
    )(scores)


def _sample_attend_kernel(pg_ref, lp_ref, q_ref, kn_ref, vn_ref, s_ref, cv_hbm, o_ref, vbuf, sems,
                          *, layer, n_heads, n_sel, dh):
    bi = pl.program_id(0)
    slot = bi % 2

    def value_copies(seq, buf_slot):
        return [pltpu.make_async_copy(cv_hbm.at[layer, pg_ref[(seq * n_heads + h) * n_sel + r], h],
                                      vbuf.at[buf_slot, h, r], sems.at[buf_slot])
                for h in range(n_heads) for r in range(n_sel)]

    @pl.when(bi == 0)
    def _():
        for c in value_copies(bi, slot):
            c.start()

    @pl.when(bi + 1 < pl.num_programs(0))
    def _():
        for c in value_copies(bi + 1, 1 - slot):
            c.start()

    q = q_ref[0]
    rows, selfs = [], []
    for h in range(n_heads):
        sl = slice(h * dh, (h + 1) * dh)
        rows.append(jnp.concatenate(
            [s_ref[0, lp_ref[(bi * n_heads + h) * n_sel + r], pl.ds(h, 1), :] for r in range(n_sel)],
            axis=1))
        selfs.append(jnp.sum(q[:, sl] * kn_ref[0][:, sl], axis=-1, keepdims=True))
    s = jnp.concatenate(rows, axis=0)
    s_self = jnp.concatenate(selfs, axis=0)
    m = jnp.maximum(jnp.max(s, axis=-1, keepdims=True), s_self)
    p = jnp.exp(s - m)
    p_self = jnp.exp(s_self - m)
    inv_l = 1.0 / (jnp.sum(p, axis=-1, keepdims=True) + p_self)

    for c in value_copies(bi, slot):
        c.wait()
    pv = [_bdot_nt(jnp.broadcast_to(p[h:h + 1, :], (8, p.shape[1])),
                   jnp.concatenate([vbuf[slot, h, r] for r in range(n_sel)], axis=1))[0:1, :]
          for h in range(n_heads)]
    for h in range(n_heads):
        sl = slice(h * dh, (h + 1) * dh)
        o_ref[0, :, sl] = (pv[h] + p_self[h:h + 1, :] * vn_ref[0][:, sl]) * inv_l[h:h + 1, :]


def _sample_attend(q, k_new, v_new, scores, cache_vt, layer, phys, logical):
    b, _, d = q.shape
    _, _, n_heads, dh, page = cache_vt.shape
    n_pages = scores.shape[1]
    n_sel = phys.shape[-1]
    row_spec = pl.BlockSpec((1, 1, d), lambda bi, pg, lp: (bi, 0, 0))
    return pl.pallas_call(
        functools.partial(_sample_attend_kernel, layer=layer, n_heads=n_heads, n_sel=n_sel, dh=dh),
        grid_spec=pltpu.PrefetchScalarGridSpec(
            num_scalar_prefetch=2, grid=(b,),
            in_specs=[row_spec, row_spec, row_spec,
                      pl.BlockSpec((1, n_pages, n_heads, page), lambda bi, pg, lp: (bi, 0, 0, 0)),
                      pl.BlockSpec(memory_space=pl.ANY)],
            out_specs=row_spec,
            scratch_shapes=[pltpu.VMEM((2, n_heads, n_sel, dh, page), F32),
                            pltpu.SemaphoreType.DMA((2,))]),
        out_shape=jax.ShapeDtypeStruct((b, 1, d), F32),
        compiler_params=_params("arbitrary"), name="moba_sample_attend",
    )(phys.reshape(-1), logical.reshape(-1), q, k_new, v_new, scores, cache_vt)


def _matmul_res_kernel(a_ref, w_ref, r_ref, o_ref):
    o_ref[...] = r_ref[...] + _bdot(a_ref[...], w_ref[...])


def _matmul_res(a, w_bf, res, *, tm):
    m, k = a.shape
    n = w_bf.shape[1]
    return pl.pallas_call(
        _matmul_res_kernel, grid=(m // tm,),
        in_specs=[pl.BlockSpec((tm, k), lambda i: (i, 0)), pl.BlockSpec((k, n), lambda i: (0, 0)),
                  pl.BlockSpec((tm, n), lambda i: (i, 0))],
        out_specs=pl.BlockSpec((tm, n), lambda i: (i, 0)),
        out_shape=jax.ShapeDtypeStruct((m, n), F32),
        compiler_params=_params("parallel"), name="matmul_residual",
    )(a, w_bf, res)


def _ffn_kernel(x_ref, nw_ref, wg_ref, wu_ref, wo_ref, *refs, final_norm):
    if final_norm:
        fw_ref, o_ref, h_scr, acc_scr = refs
    else:
        o_ref, h_scr, acc_scr = refs
    j = pl.program_id(1)

    @pl.when(j == 0)
    def _():
        h_scr[...] = _rms(x_ref[...], nw_ref[...]).astype(BF16)
        acc_scr[...] = x_ref[...]

    h = h_scr[...]
    gate = jnp.dot(h, wg_ref[...], preferred_element_type=F32)
    up = jnp.dot(h, wu_ref[...], preferred_element_type=F32)
    acc_scr[...] += _bdot(_silu(gate) * up, wo_ref[...])

    @pl.when(j == pl.num_programs(1) - 1)
    def _():
        y = acc_scr[...]
        o_ref[...] = _rms(y, fw_ref[...]) if final_norm else y


def _ffn(x, nw, w_in_bf, w_out_bf, final_w=None, *, tm, n_hidden_tiles=1):
    m, d = x.shape
    hidden = w_out_bf.shape[0]
    th = hidden // n_hidden_tiles
    assert th * n_hidden_tiles == hidden and th % 128 == 0
    mode = dict(pipeline_mode=pl.Buffered(1)) if n_hidden_tiles == 1 else {}
    in_specs = [pl.BlockSpec((tm, d), lambda i, j: (i, 0)), pl.BlockSpec((1, d), lambda i, j: (0, 0)),
                pl.BlockSpec((d, th), lambda i, j: (0, j), **mode),
                pl.BlockSpec((d, th), lambda i, j: (0, n_hidden_tiles + j), **mode),
                pl.BlockSpec((th, d), lambda i, j: (j, 0), **mode)]
    args = [x, nw.reshape(1, d), w_in_bf, w_in_bf, w_out_bf]
    if final_w is not None:
        in_specs.append(pl.BlockSpec((1, d), lambda i, j: (0, 0)))
        args.append(final_w.reshape(1, d))
    return pl.pallas_call(
        functools.partial(_ffn_kernel, final_norm=final_w is not None),
        grid=(m // tm, n_hidden_tiles), in_specs=in_specs,
        out_specs=pl.BlockSpec((tm, d), lambda i, j: (i, 0)),
        out_shape=jax.ShapeDtypeStruct((m, d), F32),
        scratch_shapes=[pltpu.VMEM((tm, d), BF16), pltpu.VMEM((tm, d), F32)],
        compiler_params=_params("parallel", "arbitrary"), name="swiglu_ffn",
    )(*args)


def _norm_matmul_kernel(x_ref, nw_ref, w_ref, o_ref, h_scr):
    @pl.when(pl.program_id(1) == 0)
    def _():
        h_scr[...] = _rms(x_ref[...], nw_ref[...]).astype(BF16)

    o_ref[...] = jnp.dot(h_scr[...], w_ref[...], preferred_element_type=F32)


def _norm_matmul(x, nw, w_bf, *, tm, tn):
    m, d = x.shape
    n = w_bf.shape[1]
    return pl.pallas_call(
        _norm_matmul_kernel, grid=(m // tm, n // tn),
        in_specs=[pl.BlockSpec((tm, d), lambda i, j: (i, 0)), pl.BlockSpec((1, d), lambda i, j: (0, 0)),
                  pl.BlockSpec((d, tn), lambda i, j: (0, j))],
        out_specs=pl.BlockSpec((tm, tn), lambda i, j: (i, j)),
        out_shape=jax.ShapeDtypeStruct((m, n), F32),
        scratch_shapes=[pltpu.VMEM((tm, d), BF16)],
        compiler_params=_params("parallel", "arbitrary"), name="norm_matmul",
    )(x, nw.reshape(1, d), w_bf)


def _gates_kernel(x_ref, nw_ref, w_ref, alog_ref, dtb_ref, beta_ref, g_ref, gc_ref, *, nh, chunk):
    h = _rms(x_ref[...], nw_ref[...])
    ba = jnp.dot(h, w_ref[...], precision=HI, preferred_element_type=F32)
    b_in, a_in = ba[:, 0:nh], ba[:, LANES:LANES + nh]
    z = a_in + dtb_ref[...]
    softplus = jnp.maximum(z, 0.0) + jnp.log1p(jnp.exp(-jnp.abs(z)))
    g = -jnp.exp(alog_ref[...]) * softplus
    beta_ref[...] = jax.nn.sigmoid(b_in)
    g_ref[...] = g
    tm = g.shape[0]
    if chunk == 1:
        gc_ref[...] = g
    else:
        r = lax.broadcasted_iota(jnp.int32, (chunk, chunk), 0)
        c = lax.broadcasted_iota(jnp.int32, (chunk, chunk), 1)
        tri = jnp.where(r >= c, 1.0, 0.0)
        for i in range(tm // chunk):
            gc_ref[i * chunk:(i + 1) * chunk, :] = jnp.dot(
                tri, g[i * chunk:(i + 1) * chunk, :], precision=HI, preferred_element_type=F32)


def _gates(x, nw, w_ba, a_log, dt_bias, *, tm, chunk):
    m, d = x.shape
    nh = a_log.shape[0]
    spec = pl.BlockSpec((tm, nh), lambda i: (i, 0))
    small = pl.BlockSpec((1, nh), lambda i: (0, 0))
    return pl.pallas_call(
        functools.partial(_gates_kernel, nh=nh, chunk=chunk), grid=(m // tm,),
        in_specs=[pl.BlockSpec((tm, d), lambda i: (i, 0)), pl.BlockSpec((1, d), lambda i: (0, 0)),
                  pl.BlockSpec(w_ba.shape, lambda i: (0, 0)), small, small],
        out_specs=[spec, spec, spec],
        out_shape=[jax.ShapeDtypeStruct((m, nh), F32)] * 3,
        compiler_params=_params("parallel"), name="delta_gates",
    )(x, nw.reshape(1, d), w_ba, a_log.reshape(1, nh), dt_bias.reshape(1, nh))


def _post_conv(conv, q_ref, k_ref, v_ref, *, key_dim, dk):
    y = _silu(conv)
    for j in range(key_dim // dk):
        for ref, off, scale in ((q_ref, 0, dk ** -0.5), (k_ref, key_dim, 1.0)):
            t = y[:, off + j * dk:off + (j + 1) * dk]
            inv = lax.rsqrt(jnp.sum(t * t, axis=-1, keepdims=True) + NORM_EPS)
            ref[0, :, j * dk:(j + 1) * dk] = t * (inv * scale)
    v_ref[0] = y[:, 2 * key_dim:]


SUBLANES = 8


def _delta_inproj_kernel(x_ref, nw_ref, w_ref, cw_ref, q_ref, k_ref, v_ref, z_ref, tail_ref, halo,
                         *, key_dim, dk, tiles_per_seq):
    i = pl.program_id(0)
    width, conv_dim = cw_ref.shape
    tm = x_ref.shape[0]

    @pl.when(i % tiles_per_seq == 0)
    def _():
        halo[...] = jnp.zeros_like(halo)

    h = _rms(x_ref[...], nw_ref[...]).astype(BF16)
    row = lax.broadcasted_iota(jnp.int32, (SUBLANES, key_dim), 0)
    for j in range(w_ref.shape[1] // key_dim):
        cols = slice(j * key_dim, (j + 1) * key_dim)
        acc = jnp.dot(h, w_ref[:, cols], preferred_element_type=F32)
        if j * key_dim >= conv_dim:
            z_ref[:, j * key_dim - conv_dim:(j + 1) * key_dim - conv_dim] = acc
            continue
        prev = halo[:, cols]
        conv = acc * cw_ref[width - 1:width, cols]
        for s in range(1, width):
            rolled = pltpu.roll(acc, s, 0)
            head = jnp.where(row < s, pltpu.roll(prev, s, 0), rolled[0:SUBLANES])
            conv = conv + jnp.concatenate([head, rolled[SUBLANES:]], axis=0) * cw_ref[width - 1 - s:width - s, cols]
        halo[:, cols] = acc[tm - SUBLANES:tm]
        tail_ref[0, :, cols] = acc[tm - SUBLANES:tm]
        y = _silu(conv)
        if j >= 2:
            v_ref[:, (j - 2) * key_dim:(j - 1) * key_dim] = y
            continue
        ref, scale = (q_ref, dk ** -0.5) if j == 0 else (k_ref, 1.0)
        for hd in range(key_dim // dk):
            t = y[:, hd * dk:(hd + 1) * dk]
            inv = lax.rsqrt(jnp.sum(t * t, axis=-1, keepdims=True) + NORM_EPS)
            ref[:, hd * dk:(hd + 1) * dk] = t * (inv * scale)


def _delta_inproj(x, nw, w_bf, conv_w, *, seq_len, key_dim, dk, tm):
    m, d = x.shape
    width, conv_dim = conv_w.shape
    n = w_bf.shape[1]
    val_dim = conv_dim - 2 * key_dim
    assert n == conv_dim + val_dim and val_dim % key_dim == 0 and seq_len % tm == 0 and tm > SUBLANES
    tiles_per_seq = seq_len // tm
    rows = lambda c: pl.BlockSpec((tm, c), lambda i: (i, 0))
    return pl.pallas_call(
        functools.partial(_delta_inproj_kernel, key_dim=key_dim, dk=dk, tiles_per_seq=tiles_per_seq),
        grid=(m // tm,),
        in_specs=[rows(d), pl.BlockSpec((1, d), lambda i: (0, 0)),
                  pl.BlockSpec((d, n), lambda i: (0, 0), pipeline_mode=pl.Buffered(1)),
                  pl.BlockSpec((width, conv_dim), lambda i: (0, 0))],
        out_specs=[rows(key_dim), rows(key_dim), rows(val_dim), rows(val_dim),
                   pl.BlockSpec((1, SUBLANES, conv_dim), lambda i: (i // tiles_per_seq, 0, 0))],
        out_shape=[jax.ShapeDtypeStruct((m, key_dim), F32), jax.ShapeDtypeStruct((m, key_dim), F32),
                   jax.ShapeDtypeStruct((m, val_dim), F32), jax.ShapeDtypeStruct((m, val_dim), F32),
                   jax.ShapeDtypeStruct((m // seq_len, SUBLANES, conv_dim), F32)],
        scratch_shapes=[pltpu.VMEM((SUBLANES, conv_dim), F32)],
        compiler_params=_params("arbitrary"), name="delta_inproj_conv",
    )(x, nw.reshape(1, d), w_bf, conv_w)


def _conv_step_kernel(hist_ref, cur_ref, w_ref, q_ref, k_ref, v_ref, *, key_dim, dk, width):
    conv = cur_ref[...] * w_ref[width - 1:width, :]
    for i in range(width - 1):
        conv = conv + hist_ref[i] * w_ref[i:i + 1, :]
    _post_conv(conv, q_ref, k_ref, v_ref, key_dim=key_dim, dk=dk)


def _conv_step(hist_t, cur, conv_w, *, key_dim, dk):
    _, b, c = hist_t.shape
    width = conv_w.shape[0]
    val_dim = c - 2 * key_dim
    return pl.pallas_call(
        functools.partial(_conv_step_kernel, key_dim=key_dim, dk=dk, width=width),
        out_shape=[jax.ShapeDtypeStruct((1, b, key_dim), F32), jax.ShapeDtypeStruct((1, b, key_dim), F32),
                   jax.ShapeDtypeStruct((1, b, val_dim), F32)],
        compiler_params=pltpu.CompilerParams(vmem_limit_bytes=VMEM_LIMIT_BYTES), name="delta_conv_step",
    )(hist_t, cur, conv_w)


def _unit_lower_inverse(mats, row, col):
    c = mats[0].shape[0]
    shift = INV_BASE.bit_length() - 1
    same = lax.shift_right_logical(row, shift) == lax.shift_right_logical(col, shift)
    eye = jnp.where(row == col, 1.0, 0.0)
    power = [jnp.where(same, a, 0.0) for a in mats]
    xs = [eye - p for p in power]
    for _ in range(shift - 1):
        power = [_bdot(p, p) for p in power]
        xs = [x + _bdot(x, p) for x, p in zip(xs, power)]
    size = INV_BASE
    while size < c:
        shift += 1
        wider = lax.shift_right_logical(row, shift) == lax.shift_right_logical(col, shift)
        off = wider & jnp.logical_not(same)
        ex = [_bdot(jnp.where(off, a, 0.0), x) for a, x in zip(mats, xs)]
        xs = [x - _bdot(x, e) for x, e in zip(xs, ex)]
        same = wider
        size *= 2
    return xs


def _delta_chunk_kernel(q_ref, k_ref, v_ref, beta_ref, g_ref, gt_ref, s0_ref,
                        o_ref, s_ref, state, *, chunk, dk, dv):
    ci = pl.program_id(1)

    @pl.when(ci == 0)
    def _():
        state[...] = s0_ref[...]

    row = lax.broadcasted_iota(jnp.int32, (chunk, chunk), 0)
    col = lax.broadcasted_iota(jnp.int32, (chunk, chunk), 1)
    causal = row >= col
    n_seq, nh = state.shape[:2]
    hk = q_ref.shape[2] // dk
    rep = nh // hk
    chains = [(s, h) for s in range(n_seq) for h in range(nh)]
    groups = [(s, j) for s in range(n_seq) for j in range(hk)]
    q = {(s, j): q_ref[s, :, j * dk:(j + 1) * dk] for s, j in groups}
    k = {(s, j): k_ref[s, :, j * dk:(j + 1) * dk] for s, j in groups}
    kt = {c: k[c].T for c in groups}
    kk = {c: _bdot(k[c], kt[c]) for c in groups}
    qk = {c: _bdot(q[c], kt[c]) for c in groups}
    beta_all = [beta_ref[s] for s in range(n_seq)]
    g_all = [g_ref[s] for s in range(n_seq)]
    gt_all = [gt_ref[s, 0] for s in range(n_seq)]
    beta = {(s, h): beta_all[s][:, h:h + 1] for s, h in chains}
    gcol = {(s, h): g_all[s][:, h:h + 1] for s, h in chains}
    grow = {(s, h): gt_all[s][h:h + 1, :] for s, h in chains}
    decay = {c: jnp.where(causal, jnp.exp(jnp.where(causal, gcol[c] - grow[c], 0.0)), 0.0)
             for c in chains}
    a = [jnp.where(row > col, beta[s, h] * kk[s, h // rep] * decay[s, h], 0.0) for s, h in chains]
    t_inv = _unit_lower_inverse(a, row, col)
    t_off = {c: jnp.where(row == col, 0.0, t) for c, t in zip(chains, t_inv)}
    eg = {c: jnp.exp(gcol[c]) for c in chains}
    bv = {(s, h): beta[s, h] * v_ref[s, :, h * dv:(h + 1) * dv] for s, h in chains}
    bk = {(s, h): (beta[s, h] * eg[s, h]) * k[s, h // rep] for s, h in chains}
    u_val = {c: bv[c] + _bdot(t_off[c], bv[c]) for c in chains}
    w = {c: bk[c] + _bdot(t_off[c], bk[c]) for c in chains}
    s_old = {(s, h): state[s, h] for s, h in chains}
    u = {c: u_val[c] - _bdot(w[c], s_old[c]) for c in chains}
    o_state = {(s, h): _bdot(q[s, h // rep] * eg[s, h], s_old[s, h]) for s, h in chains}
    o_local = {(s, h): _bdot(qk[s, h // rep] * decay[s, h], u[s, h]) for s, h in chains}
    g_last = {c: gcol[c][chunk - 1:chunk, :] for c in chains}
    s_add = {(s, h): _bdot(kt[s, h // rep] * jnp.exp(g_last[s, h] - grow[s, h]), u[s, h])
             for s, h in chains}
    for s, h in chains:
        o_ref[s, :, h * dv:(h + 1) * dv] = o_state[s, h] + o_local[s, h]
        state[s, h] = s_old[s, h] * jnp.exp(g_last[s, h]) + s_add[s, h]

    @pl.when(ci == pl.num_programs(1) - 1)
    def _():
        s_ref[...] = state[...]


DELTA_SEQS_PER_STEP = 2


def _delta_chunked(qn, kn, v, beta, gc, s0, *, dk, dv, chunk):
    b, t, key_dim = qn.shape
    nh = beta.shape[-1]
    nc = t // chunk
    ns = DELTA_SEQS_PER_STEP if b % DELTA_SEQS_PER_STEP == 0 else 1
    gt = gc.reshape(b, nc, chunk, nh).transpose(0, 1, 3, 2)
    tok = lambda n: pl.BlockSpec((ns, chunk, n), lambda bi, c: (bi, c, 0))
    st = pl.BlockSpec((ns, nh, dk, dv), lambda bi, c: (bi, 0, 0, 0))
    return pl.pallas_call(
        functools.partial(_delta_chunk_kernel, chunk=chunk, dk=dk, dv=dv),
        grid=(b // ns, nc),
        in_specs=[tok(key_dim), tok(key_dim), tok(nh * dv), tok(nh), tok(nh),
                  pl.BlockSpec((ns, 1, nh, chunk), lambda bi, c: (bi, c, 0, 0)), st],
        out_specs=[tok(nh * dv), st],
        out_shape=[jax.ShapeDtypeStruct((b, t, nh * dv), F32),
                   jax.ShapeDtypeStruct((b, nh, dk, dv), F32)],
        scratch_shapes=[pltpu.VMEM((ns, nh, dk, dv), F32)],
        compiler_params=_params("parallel", "arbitrary"), name="delta_chunked",
    )(qn, kn, v, beta, gc, gt, s0)


def _delta_step_kernel(q_ref, k_ref, qc_ref, kc_ref, v_ref, beta_ref, g_ref, s0_ref, o_ref, s_ref,
                       *, dk, dv, rep):
    nh = s0_ref.shape[1]
    for hv in range(nh):
        j = hv // rep
        q_row = q_ref[0][:, j * dk:(j + 1) * dk]
        k_row = k_ref[0][:, j * dk:(j + 1) * dk]
        q_col = qc_ref[0, j * dk:(j + 1) * dk, :]
        k_col = kc_ref[0, j * dk:(j + 1) * dk, :]
        beta = beta_ref[0][:, hv:hv + 1]
        eg = jnp.exp(g_ref[0][:, hv:hv + 1])
        s_old = s0_ref[0, hv]
        k_s = jnp.sum(s_old * k_col, axis=0, keepdims=True)
        q_s = jnp.sum(s_old * q_col, axis=0, keepdims=True)
        u = beta * (v_ref[0][:, hv * dv:(hv + 1) * dv] - eg * k_s)
        qk = jnp.sum(q_row * k_row, axis=-1, keepdims=True)
        o_ref[0, :, hv * dv:(hv + 1) * dv] = eg * q_s + qk * u
        s_ref[0, hv] = s_old * eg + k_col * u


def _delta_step(qn, kn, v, beta, g, s0, *, dk, dv):
    b, _, key_dim = qn.shape
    nh = beta.shape[-1]
    rep = nh // (key_dim // dk)
    row = lambda n: pl.BlockSpec((1, 1, n), lambda bi: (bi, 0, 0))
    col = pl.BlockSpec((1, key_dim, 1), lambda bi: (bi, 0, 0))
    st = pl.BlockSpec((1, nh, dk, dv), lambda bi: (bi, 0, 0, 0))
    return pl.pallas_call(
        functools.partial(_delta_step_kernel, dk=dk, dv=dv, rep=rep), grid=(b,),
        in_specs=[row(key_dim), row(key_dim), col, col, row(nh * dv), row(nh), row(nh), st],
        out_specs=[row(nh * dv), st],
        out_shape=[jax.ShapeDtypeStruct((b, 1, nh * dv), F32), jax.ShapeDtypeStruct(s0.shape, F32)],
        compiler_params=_params("parallel"), name="delta_step",
    )(qn, kn, qn.reshape(b, key_dim, 1), kn.reshape(b, key_dim, 1), v, beta, g, s0)


def _gated_out_kernel(o_ref, z_ref, gw_ref, w_ref, r_ref, y_ref, a_scr, *, dv):
    gw = gw_ref[...]
    for hv in range(o_ref.shape[1] // dv):
        sl = slice(hv * dv, (hv + 1) * dv)
        o = o_ref[:, sl]
        inv = lax.rsqrt(jnp.mean(o * o, axis=-1, keepdims=True) + NORM_EPS)
        a_scr[:, sl] = (o * inv * gw * _silu(z_ref[:, sl])).astype(BF16)
    y_ref[...] = r_ref[...] + jnp.dot(a_scr[...], w_ref[...], preferred_element_type=F32)


def _gated_out(o, proj, z_col_block, gnorm_w, w_out_bf, res, *, tm, dv):
    m, val_dim = o.shape
    n = w_out_bf.shape[1]
    return pl.pallas_call(
        functools.partial(_gated_out_kernel, dv=dv), grid=(m // tm,),
        in_specs=[pl.BlockSpec((tm, val_dim), lambda i: (i, 0)),
                  pl.BlockSpec((tm, val_dim), lambda i: (i, z_col_block)),
                  pl.BlockSpec((1, dv), lambda i: (0, 0)),
                  pl.BlockSpec((val_dim, n), lambda i: (0, 0)),
                  pl.BlockSpec((tm, n), lambda i: (i, 0))],
        out_specs=pl.BlockSpec((tm, n), lambda i: (i, 0)),
        out_shape=jax.ShapeDtypeStruct((m, n), F32),
        scratch_shapes=[pltpu.VMEM((tm, val_dim), BF16)],
        compiler_params=_params("parallel"), name="delta_gated_out",
    )(o, proj, gnorm_w.reshape(1, dv), w_out_bf, res)


def kernel(x_prompt, x_sample, cache_k, cache_v, page_table, state_conv, state_ssm, norm_mix, w_qkv_attn, w_o_attn, w_in_delta, conv_w_delta, a_log_delta, dt_bias_delta, gnorm_delta, w_out_delta, norm_ffn, w_ffn_in, w_ffn_out, norm_final):
    b, t, d = x_prompt.shape
    sb = x_sample.shape[0]
    assert x_sample.shape[1] == 1, "the sample group advances one token per sequence"
    n_heads, head_dim = cache_k.shape[-2:]
    page = cache_k.shape[2]
    n_pages = page_table.shape[1]
    past_len = n_pages * page
    assert past_len % MOBA_BLOCK == 0 and MOBA_BLOCK % page == 0 and t % MOBA_BLOCK == 0
    depth = norm_mix.shape[0]
    assert depth == 2, "layer 0 is attention, layer 1 is the gated DeltaNet"
    nh_v, dk, dv = state_ssm.shape[2:]
    conv_dim = conv_w_delta.shape[-1]
    val_dim = nh_v * dv
    key_dim = (conv_dim - val_dim) // 2
    assert conv_dim % val_dim == 0 and t % DN_CHUNK == 0

    xp = x_prompt.reshape(b * t, d)
    xs = x_sample.reshape(sb, d)
    tm_p = _row_tile(b * t, 512)

    w_qkv = w_qkv_attn[0].astype(BF16)
    w_o = w_o_attn[0].astype(BF16)
    nb_p = t // MOBA_BLOCK
    q_p, k_p, v_p, kb_p, vt_p, km_p = _qkv_rope(
        xp, norm_mix[0], w_qkv, jnp.arange(t, dtype=jnp.int32), tm=MOBA_BLOCK, head_dim=head_dim,
        q_scale=head_dim ** -0.5 * math.log2(math.e), q_dtype=BF16, block_outputs=True)
    o_p = _moba_prompt(q_p.reshape(b, t, d), kb_p.reshape(b, t, d), vt_p.reshape(b, nb_p, d, MOBA_BLOCK),
                       km_p.reshape(b, nb_p, d), head_dim=head_dim)
    xp = _matmul_res(o_p.reshape(b * t, d), w_o, xp, tm=tm_p)

    q_s, k_s, v_s = _qkv_rope(
        xs, norm_mix[0], w_qkv, jnp.full((sb,), past_len, jnp.int32), tm=sb, head_dim=head_dim,
        q_scale=head_dim ** -0.5, q_dtype=F32, block_outputs=False)
    per_blk = MOBA_BLOCK // page
    cache_kt = cache_k.transpose(0, 1, 3, 4, 2)
    cache_vt = cache_v.transpose(0, 1, 3, 4, 2)
    scores = _sample_scores(q_s.reshape(sb, n_heads, head_dim, 1), cache_kt, 0, page_table)
    sel = _sample_select(scores, per_blk)[..., 0]
    logical = (sel.transpose(0, 2, 1)[..., None] * per_blk
               + jnp.arange(per_blk, dtype=jnp.int32)).reshape(sb, n_heads, -1)
    phys = jnp.take_along_axis(page_table, logical.reshape(sb, -1), axis=1).reshape(sb, n_heads, -1)
    o_s = _sample_attend(q_s.reshape(sb, 1, d), k_s.reshape(sb, 1, d), v_s.reshape(sb, 1, d),
                         scores, cache_vt, 0, phys, logical)
    xs = _matmul_res(o_s.reshape(sb, d), w_o, xs, tm=sb)

    xp = _ffn(xp, norm_ffn[0], w_ffn_in[0].astype(BF16), w_ffn_out[0].astype(BF16), tm=tm_p)
    xs = _ffn(xs, norm_ffn[0], w_ffn_in[0].astype(BF16), w_ffn_out[0].astype(BF16), tm=sb)

    w_in = w_in_delta[0]
    main = conv_dim + val_dim
    w_main = w_in[:, :main].astype(BF16)
    w_ba = jnp.zeros((d, 2 * LANES), F32).at[:, :nh_v].set(w_in[:, main:main + nh_v])
    w_ba = w_ba.at[:, LANES:LANES + nh_v].set(w_in[:, main + nh_v:])
    w_out = w_out_delta[0].astype(BF16)
    conv_w = conv_w_delta[0]
    z_block = conv_dim // val_dim

    qn_p, kn_p, vv_p, z_p, tail_p = _delta_inproj(xp, norm_mix[1], w_main, conv_w, seq_len=t,
                                                  key_dim=key_dim, dk=dk, tm=256)
    beta_p, _, gc_p = _gates(xp, norm_mix[1], w_ba, a_log_delta[0], dt_bias_delta[0],
                             tm=tm_p, chunk=DN_CHUNK)
    o_dp, ssm_p = _delta_chunked(qn_p.reshape(b, t, key_dim), kn_p.reshape(b, t, key_dim),
                                 vv_p.reshape(b, t, val_dim), beta_p.reshape(b, t, nh_v),
                                 gc_p.reshape(b, t, nh_v), jnp.zeros((b,) + state_ssm.shape[2:], F32),
                                 dk=dk, dv=dv, chunk=DN_CHUNK)
    xp = _gated_out(o_dp.reshape(b * t, val_dim), z_p, 0, gnorm_delta[0], w_out, xp, tm=tm_p, dv=dv)
    conv_p = tail_p[:, SUBLANES - (conv_w.shape[0] - 1):, :]

    proj_s = _norm_matmul(xs, norm_mix[1], w_main, tm=sb, tn=1024)
    beta_s, g_s, _ = _gates(xs, norm_mix[1], w_ba, a_log_delta[0], dt_bias_delta[0], tm=sb, chunk=1)
    hist_t = state_conv[0].transpose(1, 0, 2)
    cur_s = proj_s[:, :conv_dim]
    qn_s, kn_s, vv_s = _conv_step(hist_t, cur_s, conv_w, key_dim=key_dim, dk=dk)
    o_ds, ssm_s = _delta_step(qn_s.reshape(sb, 1, key_dim), kn_s.reshape(sb, 1, key_dim),
                              vv_s.reshape(sb, 1, val_dim), beta_s.reshape(sb, 1, nh_v),
                              g_s.reshape(sb, 1, nh_v), state_ssm[0], dk=dk, dv=dv)
    xs = _gated_out(o_ds.reshape(sb, val_dim), proj_s, z_block, gnorm_delta[0], w_out, xs, tm=sb, dv=dv)
    conv_s = jnp.concatenate([state_conv[0][:, 1:], cur_s[:, None, :]], axis=1)

    y_p = _ffn(xp, norm_ffn[1], w_ffn_in[1].astype(BF16), w_ffn_out[1].astype(BF16), norm_final, tm=tm_p)
    y_s = _ffn(xs, norm_ffn[1], w_ffn_in[1].astype(BF16), w_ffn_out[1].astype(BF16), norm_final, tm=sb)

    kv_p = (b, t, n_heads, head_dim)
    kv_s = (sb, 1, n_heads, head_dim)
    return (y_p.reshape(b, t, d), y_s.reshape(sb, 1, d),
            k_p.reshape(kv_p)[None], v_p.reshape(kv_p)[None],
            k_s.reshape(kv_s)[None], v_s.reshape(kv_s)[None],
            conv_p[None], ssm_p[None], conv_s[None], ssm_s[None])
```
